```python
import jax, jax.numpy as jnp
from jax import lax
import numpy as np

D_MODEL = 1024
BATCH = 8
SEQ = 8192
DEPTH = 4

CHUNK = 64
REC_CHUNK = 16
Q_BLOCK = 128
N_MEM = 256
EPS = 1e-6
MASK_VALUE = -1e30
LB_FLOOR = 1e-30

MLA_HEADS = 4
MLA_Q_LORA = 256
MLA_KV_LORA = 256
MLA_NOPE = 128
MLA_ROPE = 64
MLA_V = 128
MLA_QK = MLA_NOPE + MLA_ROPE
ROPE_THETA = 10000.0

GLA_HEADS = 4
GLA_DK = 32
GLA_DV = 64
GLA_GATE_RANK = 16
GLA_TAU = 16.0

HGRN_HEADS = 4
HGRN_DK = 64
HGRN_DV = 64

D_MIX = MLA_HEADS * MLA_V + GLA_HEADS * GLA_DV + HGRN_HEADS * HGRN_DV

IN_SIZES = (
    MLA_Q_LORA, MLA_KV_LORA, MLA_ROPE,
    GLA_HEADS * GLA_DK, GLA_HEADS * GLA_DK, GLA_HEADS * GLA_DV,
    GLA_GATE_RANK, GLA_HEADS * GLA_DV,
    HGRN_HEADS * HGRN_DK, HGRN_HEADS * HGRN_DK,
    HGRN_HEADS * HGRN_DV, HGRN_HEADS * HGRN_DV,
)
P_IN = sum(IN_SIZES)

CA_HEADS = 4
CA_HEAD_DIM = 128

D_FF = 2816
CONV_WIDTH = 3

kernel_name = 'hybrid_mla_gla_hgrn2_memory_convglu'


def rms_norm(x, gain):
    xf = x.astype(jnp.float32)
    y = xf * lax.rsqrt(jnp.mean(xf * xf, axis=-1, keepdims=True) + EPS)
    return y.astype(x.dtype) * gain


def rotary(x, cos, sin):
    x1, x2 = jnp.split(x, 2, axis=-1)
    return jnp.concatenate([x1 * cos - x2 * sin, x2 * cos + x1 * sin], axis=-1)


def chunked_gated_recurrence(q, k, v, log_decay):
    B, S, H, DK = q.shape
    DV = v.shape[-1]
    n = S // REC_CHUNK

    def to_chunks(t):
        return t.astype(jnp.float32).reshape(B, n, REC_CHUNK, H, t.shape[-1]).transpose(0, 3, 1, 2, 4)

    qc, kc, vc, gc = to_chunks(q), to_chunks(k), to_chunks(v), to_chunks(log_decay)
    b = jnp.cumsum(gc, axis=3)
    b_last = b[:, :, :, -1:, :]
    causal = jnp.tril(jnp.ones((REC_CHUNK, REC_CHUNK), dtype=jnp.float32))
    rel = b[:, :, :, :, None, :] - b[:, :, :, None, :, :]
    pair_decay = jnp.exp(jnp.minimum(rel, 0.0)) * causal[:, :, None]
    scores = jnp.einsum('bhnid,bhnjd,bhnijd->bhnij', qc, kc, pair_decay)
    o_intra = jnp.einsum('bhnij,bhnjv->bhniv', scores, vc)

    updates = jnp.einsum('bhnjd,bhnjv->bhndv', kc * jnp.exp(b_last - b), vc)
    decays = jnp.exp(b_last[:, :, :, 0, :])

    def step(state, inp):
        decay, upd = inp
        return decay[..., None] * state + upd, state

    init = jnp.zeros((B, H, DK, DV), jnp.float32)
    _, starts = lax.scan(step, init, (jnp.moveaxis(decays, 2, 0), jnp.moveaxis(updates, 2, 0)))
    starts = jnp.moveaxis(starts, 0, 2)
    o_inter = jnp.einsum('bhnid,bhndv->bhniv', qc * jnp.exp(b), starts)
    o = (o_intra + o_inter).transpose(0, 2, 3, 1, 4).reshape(B, S, H, DV)
    return o.astype(v.dtype)


def mla_mixer(q_lat, kv_lat, k_rope, cos, sin, q_lat_norm, w_q_up, kv_lat_norm, w_kv_up, q_norm, k_norm):
    B, S, _ = q_lat.shape
    H = MLA_HEADS
    q = (rms_norm(q_lat, q_lat_norm) @ w_q_up).reshape(B, S, H, MLA_QK)
    kv = (rms_norm(kv_lat, kv_lat_norm) @ w_kv_up).reshape(B, S, H, MLA_NOPE + MLA_V)
    k_nope, v = kv[..., :MLA_NOPE], kv[..., MLA_NOPE:]
    k = jnp.concatenate([k_nope, jnp.broadcast_to(k_rope[:, :, None, :], (B, S, H, MLA_ROPE))], axis=-1)
    q = rms_norm(q, q_norm)
    k = rms_norm(k, k_norm)
    q = jnp.concatenate([q[..., :MLA_NOPE], rotary(q[..., MLA_NOPE:], cos, sin)], axis=-1)
    k = jnp.concatenate([k[..., :MLA_NOPE], rotary(k[..., MLA_NOPE:], cos, sin)], axis=-1)

    n_blocks = S // Q_BLOCK
    q_blocks = jnp.moveaxis(q.reshape(B, n_blocks, Q_BLOCK, H, MLA_QK), 1, 0)
    key_chunk = jnp.arange(S) // CHUNK
    scale = MLA_QK ** -0.5

    def attend(args):
        qb, blk = args
        s = jnp.einsum('bqhd,bkhd->bhqk', qb, k).astype(jnp.float32) * scale
        q_chunk = (blk * Q_BLOCK + jnp.arange(Q_BLOCK)) // CHUNK
        s = jnp.where(key_chunk[None, :] <= q_chunk[:, None], s, MASK_VALUE)
        p = jax.nn.softmax(s, axis=-1).astype(v.dtype)
        return jnp.einsum('bhqk,bkhd->bqhd', p, v)

    out = lax.map(attend, (q_blocks, jnp.arange(n_blocks)))
    return jnp.moveaxis(out, 0, 1).reshape(B, S, H * MLA_V)


def gla_mixer(q, k, v, gate_lr, r, w_gate, b_gate, out_norm):
    B, S, _ = q.shape
    H = GLA_HEADS
    log_alpha = jax.nn.log_sigmoid((gate_lr @ w_gate + b_gate).astype(jnp.float32)) / GLA_TAU
    o = chunked_gated_recurrence(
        q.reshape(B, S, H, GLA_DK) * (GLA_DK ** -0.5),
        k.reshape(B, S, H, GLA_DK),
        v.reshape(B, S, H, GLA_DV),
        log_alpha.reshape(B, S, H, GLA_DK))
    o = rms_norm(o, out_norm) * jax.nn.silu(r.reshape(B, S, H, GLA_DV))
    return o.reshape(B, S, H * GLA_DV)


def hgrn2_mixer(q, f_logit, i, g, lower_bound, out_norm):
    B, S, _ = q.shape
    H = HGRN_HEADS
    z = f_logit.astype(jnp.float32)
    log_f = jnp.logaddexp(jnp.log(jnp.maximum(lower_bound, LB_FLOOR)),
                          jnp.log1p(-lower_bound) + jax.nn.log_sigmoid(z))
    log_f = jnp.minimum(log_f, 0.0)
    one_minus_f = (1.0 - lower_bound) * jax.nn.sigmoid(-z)
    o = chunked_gated_recurrence(
        jax.nn.silu(q).reshape(B, S, H, HGRN_DK) * (HGRN_DK ** -0.5),
        one_minus_f.reshape(B, S, H, HGRN_DK),
        i.reshape(B, S, H, HGRN_DV),
        log_f.reshape(B, S, H, HGRN_DK))
    o = rms_norm(o, out_norm) * jax.nn.silu(g.reshape(B, S, H, HGRN_DV))
    return o.reshape(B, S, H * HGRN_DV)


def memory_cross_attention(h_norm, mem_norm, w_q, w_kv, q_norm, k_norm, w_out):
    B, S, _ = h_norm.shape
    M = mem_norm.shape[1]
    q = (h_norm @ w_q).reshape(B, S, CA_HEADS, CA_HEAD_DIM)
    kv = (mem_norm @ w_kv).reshape(B, M, CA_HEADS, 2 * CA_HEAD_DIM)
    k, v = kv[..., :CA_HEAD_DIM], kv[..., CA_HEAD_DIM:]
    q = rms_norm(q, q_norm)
    k = rms_norm(k, k_norm)
    s = jnp.einsum('bshd,bmhd->bhsm', q, k).astype(jnp.float32) * (CA_HEAD_DIM ** -0.5)
    p = jax.nn.softmax(s, axis=-1).astype(v.dtype)
    o = jnp.einsum('bhsm,bmhd->bshd', p, v).reshape(B, S, CA_HEADS * CA_HEAD_DIM)
    return o @ w_out


def conv_glu_ffn(h_norm, w_in, conv_w, conv_b, w_out):
    gate, up = jnp.split(h_norm @ w_in, 2, axis=-1)
    S = gate.shape[1]
    padded = jnp.pad(gate, ((0, 0), (CONV_WIDTH - 1, 0), (0, 0)))
    conv = conv_b
    for t in range(CONV_WIDTH):
        conv = conv + padded[:, t:t + S, :] * conv_w[t]
    return (jax.nn.silu(conv) * up) @ w_out


def setup_inputs(seed: int = 0) -> dict:
    key = jax.random.key(seed)
    ks = iter(jax.random.split(key, 40))
    f32 = jnp.float32
    out_scale = (2 * DEPTH) ** -0.5

    def dense(shape, fan_in, scale=1.0):
        return jax.random.normal(next(ks), shape, f32) * (scale * fan_in ** -0.5)

    def gain(shape):
        return 1.0 + 0.02 * jax.random.normal(next(ks), shape, f32)

    def small(shape):
        return 0.01 * jax.random.normal(next(ks), shape, f32)

    L = DEPTH
    x = jax.random.normal(next(ks), (BATCH, SEQ, D_MODEL), f32)
    mem = jax.random.normal(next(ks), (BATCH, N_MEM, D_MODEL), f32)
    start = jax.random.randint(next(ks), (BATCH, 1), 0, 4096, dtype=jnp.int32)
    positions = (start + jnp.arange(SEQ, dtype=jnp.int32)[None, :]).astype(jnp.int32)
    return {
        'x': x,
        'mem': mem,
        'positions': positions,
        'norm_mix': gain((L, D_MODEL)),
        'w_in': dense((L, D_MODEL, P_IN), D_MODEL),
        'mla_q_lat_norm': gain((L, MLA_Q_LORA)),
        'w_q_up': dense((L, MLA_Q_LORA, MLA_HEADS * MLA_QK), MLA_Q_LORA),
        'mla_kv_lat_norm': gain((L, MLA_KV_LORA)),
        'w_kv_up': dense((L, MLA_KV_LORA, MLA_HEADS * (MLA_NOPE + MLA_V)), MLA_KV_LORA),
        'mla_q_norm': gain((L, MLA_QK)),
        'mla_k_norm': gain((L, MLA_QK)),
        'gla_w_gate': dense((L, GLA_GATE_RANK, GLA_HEADS * GLA_DK), GLA_GATE_RANK),
        'gla_b_gate': small((L, GLA_HEADS * GLA_DK)),
        'gla_out_norm': gain((L, GLA_DV)),
        'hgrn_lb_logits': 0.1 * jax.random.normal(next(ks), (L, HGRN_HEADS * HGRN_DK), f32),
        'hgrn_out_norm': gain((L, HGRN_DV)),
        'w_mix_out': dense((L, D_MIX, D_MODEL), D_MIX, out_scale),
        'norm_ca': gain((L, D_MODEL)),
        'norm_mem': gain((L, D_MODEL)),
        'w_ca_q': dense((L, D_MODEL, CA_HEADS * CA_HEAD_DIM), D_MODEL),
        'w_ca_kv': dense((L, D_MODEL, 2 * CA_HEADS * CA_HEAD_DIM), D_MODEL),
        'ca_q_norm': gain((L, CA_HEAD_DIM)),
        'ca_k_norm': gain((L, CA_HEAD_DIM)),
        'w_ca_out': dense((L, CA_HEADS * CA_HEAD_DIM, D_MODEL), CA_HEADS * CA_HEAD_DIM, out_scale),
        'norm_ffn': gain((L, D_MODEL)),
        'w_ffn_in': dense((L, D_MODEL, 2 * D_FF), D_MODEL),
        'ffn_conv_w': dense((L, CONV_WIDTH, D_FF), CONV_WIDTH),
        'ffn_conv_b': small((L, D_FF)),
        'w_ffn_out': dense((L, D_FF, D_MODEL), D_FF, out_scale),
    }


def reference(x, mem, positions, norm_mix, w_in, mla_q_lat_norm, w_q_up, mla_kv_lat_norm, w_kv_up,
              mla_q_norm, mla_k_norm, gla_w_gate, gla_b_gate, gla_out_norm, hgrn_lb_logits, hgrn_out_norm,
              w_mix_out, norm_ca, norm_mem, w_ca_q, w_ca_kv, ca_q_norm, ca_k_norm, w_ca_out,
              norm_ffn, w_ffn_in, ffn_conv_w, ffn_conv_b, w_ffn_out):
    gamma = jax.nn.softmax(hgrn_lb_logits.astype(jnp.float32), axis=0)
    lower_bounds = jnp.cumsum(gamma, axis=0) - gamma[0]

    inv_freq = ROPE_THETA ** (-jnp.arange(0, MLA_ROPE, 2, dtype=jnp.float32) / MLA_ROPE)
    ang = positions.astype(jnp.float32)[..., None] * inv_freq
    cos = jnp.cos(ang)[:, :, None, :].astype(x.dtype)
    sin = jnp.sin(ang)[:, :, None, :].astype(x.dtype)

    split_points = np.cumsum(IN_SIZES)[:-1].tolist()
    h = x
    for l in range(DEPTH):
        n = rms_norm(h, norm_mix[l])
        (a_q_lat, a_kv_lat, a_k_rope, b_q, b_k, b_v, b_gate_lr, b_r,
         c_q, c_f, c_i, c_g) = jnp.split(n @ w_in[l], split_points, axis=-1)
        y_a = mla_mixer(a_q_lat, a_kv_lat, a_k_rope, cos, sin, mla_q_lat_norm[l], w_q_up[l],
                        mla_kv_lat_norm[l], w_kv_up[l], mla_q_norm[l], mla_k_norm[l])
        y_b = gla_mixer(b_q, b_k, b_v, b_gate_lr, b_r, gla_w_gate[l], gla_b_gate[l], gla_out_norm[l])
        y_c = hgrn2_mixer(c_q, c_f, c_i, c_g, lower_bounds[l], hgrn_out_norm[l])
        h = h + jnp.concatenate([y_a, y_b, y_c], axis=-1) @ w_mix_out[l]
        h = h + memory_cross_attention(rms_norm(h, norm_ca[l]), rms_norm(mem, norm_mem[l]),
                                       w_ca_q[l], w_ca_kv[l], ca_q_norm[l], ca_k_norm[l], w_ca_out[l])
        h = h + conv_glu_ffn(rms_norm(h, norm_ffn[l]), w_ffn_in[l], ffn_conv_w[l], ffn_conv_b[l], w_ffn_out[l])
    return h
```

```python
import functools

import numpy as np
import jax
import jax.numpy as jnp
from jax import lax
from jax.experimental import pallas as pl
from jax.experimental.pallas import tpu as pltpu

F32 = jnp.float32
BF16 = jnp.bfloat16

EPS = 1e-6
MASK_VALUE = -1e30
LB_FLOOR = 1e-30
CHUNK = 64
ROPE_THETA = 10000.0

HEADS = 4
MLA_Q_LORA = 256
MLA_KV_LORA = 256
MLA_NOPE = 128
MLA_ROPE = 64
MLA_V = 128
MLA_QK = MLA_NOPE + MLA_ROPE
MLA_QK_PAD = 256
GLA_DK = 32
GLA_DV = 64
GLA_RANK = 16
GLA_TAU = 16.0
HGRN_DK = 64
HGRN_DV = 64
CA_DIM = 128
N_A = 640
N_B = 896
N_C = 1024

LANES = 128
REC_CHUNK = 128
REC_BLOCK = 16
VMEM_LIMIT = 56 * 1024 * 1024

NT_DIMS = (((1,), (1,)), ((), ()))
TN_DIMS = (((0,), (0,)), ((), ()))


def _dot(a, b):
    return jnp.dot(a, b, preferred_element_type=F32)


def _dot_nt(a, b):
    return lax.dot_general(a, b, NT_DIMS, preferred_element_type=F32)


def _rms(x, gain, dim=None):
    dim = x.shape[-1] if dim is None else dim
    ms = jnp.sum(x * x, axis=-1, keepdims=True) * (1.0 / dim)
    return x * lax.rsqrt(ms + EPS) * gain


def _sigmoid(x):
    return 1.0 / (1.0 + jnp.exp(-x))


def _log_sigmoid(x):
    return jnp.minimum(x, 0.0) - jnp.log1p(jnp.exp(-jnp.abs(x)))


def _params(*sem):
    return pltpu.CompilerParams(dimension_semantics=sem, vmem_limit_bytes=VMEM_LIMIT)


def _const_spec(shape):
    nd = len(shape)
    return pl.BlockSpec(shape, lambda *_: (0,) * nd)


def _memkv_kernel(mem_ref, g_ref, w_ref, gk_ref, k_ref, v_ref):
    n = _rms(mem_ref[0], g_ref[0]).astype(BF16)
    kv = _dot(n, w_ref[0])
    for h in range(HEADS):
        base = 2 * CA_DIM * h
        k_ref[0, 0, h] = _rms(kv[:, base:base + CA_DIM], gk_ref[0]).astype(BF16)
        v_ref[0, 0, h] = kv[:, base + CA_DIM:base + 2 * CA_DIM].astype(BF16)


def _memkv(mem, norm_mem, w_ca_kv, ca_k_norm):
    depth, d = norm_mem.shape
    b, m, _ = mem.shape
    out = jax.ShapeDtypeStruct((depth, b, HEADS, m, CA_DIM), BF16)
    spec_o = pl.BlockSpec((1, 1, HEADS, m, CA_DIM), lambda l, i: (l, i, 0, 0, 0))
    return pl.pallas_call(
        _memkv_kernel,
        grid=(depth, b),
        in_specs=[
            pl.BlockSpec((1, m, d), lambda l, i: (i, 0, 0)),
            pl.BlockSpec((1, 1, d), lambda l, i: (l, 0, 0)),
            pl.BlockSpec((1, d, w_ca_kv.shape[-1]), lambda l, i: (l, 0, 0)),
            pl.BlockSpec((1, 1, CA_DIM), lambda l, i: (l, 0, 0)),
        ],
        out_specs=[spec_o, spec_o],
        out_shape=[out, out],
        compiler_params=_params("arbitrary", "arbitrary"),
        name="memkv",
    )(mem, norm_mem.reshape(depth, 1, d), w_ca_kv, ca_k_norm.reshape(depth, 1, CA_DIM))


def _rope(x, cos, sin):
    return x * cos + pltpu.roll(x, 64, 1) * sin


def _inproj_kernel(h_ref, g_ref, w_ref, gql_ref, wq_ref, gkvl_ref, wkv_ref, gq_ref, gk_ref,
                   cos_ref, sin_ref, q_ref, k_ref, v_ref, pb_ref, pc_ref):
    n = _rms(h_ref[0], g_ref[...]).astype(BF16)
    pa = _dot(n, w_ref[:, :N_A])
    pb_ref[0] = _dot(n, w_ref[:, N_A:N_A + N_B])
    pc_ref[0] = _dot(n, w_ref[:, N_A + N_B:])
    cos = cos_ref[0]
    sin = sin_ref[0]
    qf = _dot(_rms(pa[:, :MLA_Q_LORA], gql_ref[...]).astype(BF16), wq_ref[...])
    kvf = _dot(_rms(pa[:, MLA_Q_LORA:MLA_Q_LORA + MLA_KV_LORA], gkvl_ref[...]).astype(BF16),
               wkv_ref[...])
    kr = pa[:, MLA_Q_LORA + MLA_KV_LORA:]
    kr_ss = jnp.sum(kr * kr, axis=-1, keepdims=True)
    gq = gq_ref[...]
    gk = gk_ref[...]
    scale = MLA_QK ** -0.5
    for h in range(HEADS):
        base = MLA_QK_PAD * h
        qn = _rms(qf[:, base:base + MLA_QK_PAD], gq, MLA_QK) * scale
        q_ref[0, h, :, :MLA_NOPE] = qn[:, :MLA_NOPE].astype(BF16)
        q_ref[0, h, :, MLA_NOPE:] = _rope(qn[:, MLA_NOPE:], cos, sin).astype(BF16)
        kn = kvf[:, base:base + MLA_NOPE]
        ms = (jnp.sum(kn * kn, axis=-1, keepdims=True) + kr_ss) * (1.0 / MLA_QK)
        rs = lax.rsqrt(ms + EPS)
        k_ref[0, h, :, :MLA_NOPE] = (kn * rs * gk[:, :MLA_NOPE]).astype(BF16)
        k_ref[0, h, :, MLA_NOPE:] = _rope(kr * rs * gk[:, MLA_NOPE:], cos, sin).astype(BF16)
        v_ref[0, h] = kvf[:, base + MLA_NOPE:base + MLA_NOPE + MLA_V].astype(BF16)


def _inproj(h, g, w, gql, wq, gkvl, wkv, gq, gk, cos, sin, tm):
    b, s, d = h.shape
    grid = (b, s // tm)
    tok = lambda last: pl.BlockSpec((1, tm, last), lambda i, t: (i, t, 0))
    head = lambda last: pl.BlockSpec((1, HEADS, tm, last), lambda i, t: (i, 0, t, 0))
    return pl.pallas_call(
        _inproj_kernel,
        grid=grid,
        in_specs=[tok(d), _const_spec(g.shape), _const_spec(w.shape), _const_spec(gql.shape),
                  _const_spec(wq.shape), _const_spec(gkvl.shape), _const_spec(wkv.shape),
                  _const_spec(gq.shape), _const_spec(gk.shape), tok(LANES), tok(LANES)],
        out_specs=[head(MLA_QK_PAD), head(MLA_QK_PAD), head(MLA_V), tok(N_B), tok(N_C)],
        out_shape=[jax.ShapeDtypeStruct((b, HEADS, s, MLA_QK_PAD), BF16),
                   jax.ShapeDtypeStruct((b, HEADS, s, MLA_QK_PAD), BF16),
                   jax.ShapeDtypeStruct((b, HEADS, s, MLA_V), BF16),
                   jax.ShapeDtypeStruct((b, s, N_B), F32),
                   jax.ShapeDtypeStruct((b, s, N_C), F32)],
        compiler_params=_params("arbitrary", "arbitrary"),
        name="inproj",
    )(h, g, w, gql, wq, gkvl, wkv, gq, gk, cos, sin)


def _mla_attn_kernel(q_ref, k_ref, v_ref, o_ref, m_sc, l_sc, acc_sc, *, tq, tk):
    qi = pl.program_id(2)
    q = q_ref[0, 0]
    m_sc[...] = jnp.full(m_sc.shape, MASK_VALUE, F32)
    l_sc[...] = jnp.zeros(l_sc.shape, F32)
    acc_sc[...] = jnp.zeros(acc_sc.shape, F32)

    def step(j, masked):
        start = pl.multiple_of(j * tk, tk)
        s = _dot_nt(q, k_ref[0, 0, pl.ds(start, tk), :])
        if masked:
            qc = (qi * tq + lax.broadcasted_iota(jnp.int32, (tq, tk), 0)) // CHUNK
            kc = (j * tk + lax.broadcasted_iota(jnp.int32, (tq, tk), 1)) // CHUNK
            s = jnp.where(kc <= qc, s, MASK_VALUE)
        m_prev = m_sc[...]
        m_new = jnp.maximum(m_prev, jnp.max(s, axis=-1, keepdims=True))
        alpha = jnp.exp(m_prev - m_new)
        p = jnp.exp(s - m_new)
        l_sc[...] = alpha * l_sc[...] + jnp.sum(p, axis=-1, keepdims=True)
        acc_sc[...] = alpha * acc_sc[...] + _dot(p.astype(BF16), v_ref[0, 0, pl.ds(start, tk), :])
        m_sc[...] = m_new

    n_diag = tq // tk
    n_full = qi * n_diag

    def body(j, carry):
        step(j, False)
        return carry

    lax.fori_loop(0, n_full, body, 0)
    for d in range(n_diag):
        step(n_full + d, True)
    o_ref[0] = (acc_sc[...] / l_sc[...]).astype(BF16)


def _mla_attn(q, k, v, tq, tk):
    b, nh, s, dq = q.shape
    dv = v.shape[-1]
    kernel = functools.partial(_mla_attn_kernel, tq=tq, tk=tk)
    return pl.pallas_call(
        kernel,
        grid=(b, nh, s // tq),
        in_specs=[pl.BlockSpec((1, 1, tq, dq), lambda i, h, t: (i, h, t, 0)),
                  pl.BlockSpec((1, 1, s, dq), lambda i, h, t: (i, h, 0, 0)),
                  pl.BlockSpec((1, 1, s, dv), lambda i, h, t: (i, h, 0, 0))],
        out_specs=pl.BlockSpec((1, tq, dv), lambda i, h, t: (i, t, h)),
        out_shape=jax.ShapeDtypeStruct((b, s, nh * dv), BF16),
        scratch_shapes=[pltpu.VMEM((tq, 1), F32), pltpu.VMEM((tq, 1), F32), pltpu.VMEM((tq, dv), F32)],
        compiler_params=_params("arbitrary", "arbitrary", "arbitrary"),
        name="mla_attn",
    )(q, k, v)


def _rec_consts(dk, dv):
    hk, hv = HEADS * dk, HEADS * dv
    c, blk = REC_CHUNK, REC_BLOCK
    head_k = np.arange(hk) // dk
    head_v = np.arange(hv) // dv
    head_r = np.repeat(np.arange(HEADS), blk)
    ltri = (np.arange(c)[:, None] >= np.arange(c)[None, :])
    ind = head_k[:, None] == head_v[None, :]
    indv = (head_v[:, None] == head_v[None, :]) / dv
    hm_k = head_r[:, None] == head_k[None, :]
    hm_v = head_r[:, None] == head_v[None, :]
    return (jnp.asarray(ltri, BF16), jnp.asarray(ind, BF16), jnp.asarray(indv, BF16),
            jnp.asarray(hm_k, F32), jnp.asarray(hm_v, F32), jnp.asarray(ind.T, F32))


def _rec_chunk(q, k, v, g, st_ref, p_sc, o_sc, ltri_ref, ind_ref, hmk_ref, hmv_ref, hms_ref):
    c, hk = q.shape
    hv = v.shape[1]
    blk = REC_BLOCK
    nb = c // blk
    g_hi = g.astype(BF16)
    g_lo = (g - g_hi.astype(F32)).astype(BF16)
    ltri = ltri_ref[...]
    b = _dot(ltri, g_hi) + _dot(ltri, g_lo)

    b3 = b.reshape(nb, blk, hk)
    q3 = q.reshape(nb, blk, hk)
    k3 = k.reshape(nb, blk, hk)
    v3 = v.reshape(nb, blk, hv)
    row = lax.broadcasted_iota(jnp.int32, (nb, blk, hk), 1)
    for jj in range(blk):
        e = jnp.exp(jnp.minimum(b3 - b3[:, jj:jj + 1, :], 0.0))
        p = jnp.where(row >= jj, q3 * k3[:, jj:jj + 1, :] * e, 0.0)
        p_sc[jj * c:(jj + 1) * c, :] = p.reshape(c, hk).astype(BF16)
    r = _dot(p_sc[...], ind_ref[...])
    o = jnp.zeros((nb, blk, hv), F32)
    for jj in range(blk):
        o = o + r[jj * c:(jj + 1) * c, :].reshape(nb, blk, hv) * v3[:, jj:jj + 1, :]

    st = st_ref[...]
    o = o.reshape(c, hv) + _dot_nt((q * jnp.exp(b)).astype(BF16), st.astype(BF16))
    o_sc[...] = o

    hmk = hmk_ref[...]
    hmv = hmv_ref[...]
    for jb in range(nb - 1):
        lo, r0 = blk * jb, blk * (jb + 1)
        b_end = b[r0 - 1:r0, :]
        qj = (q[r0:, :] * jnp.exp(b[r0:, :] - b_end)).astype(BF16)
        kj = k[lo:r0, :] * jnp.exp(b_end - b[lo:r0, :])
        kbd = (jnp.concatenate([kj] * HEADS, axis=0) * hmk).astype(BF16)
        vbd = (jnp.concatenate([v[lo:r0, :]] * HEADS, axis=0) * hmv).astype(BF16)
        a = _dot_nt(qj, kbd)
        o_sc[r0:, :] += _dot(a.astype(BF16), vbd)

    b_last = b[c - 1:c, :]
    kh = (k * jnp.exp(b_last - b)).astype(BF16)
    upd = lax.dot_general(v.astype(BF16), kh, TN_DIMS, preferred_element_type=F32)
    st_ref[...] = st * jnp.exp(b_last) + upd * hms_ref[...]


def _head_rms_gate(o, gate, gain, indv_ref):
    ms = _dot((o * o).astype(BF16), indv_ref[...])
    return o * lax.rsqrt(ms + EPS) * gain * (gate * _sigmoid(gate))


def _gla_kernel(pb_ref, wg_ref, bg_ref, gn_ref, ltri_ref, ind_ref, indv_ref, hmk_ref, hmv_ref, hms_ref,
                y_ref, st_ref, p_sc, o_sc, *, n_chunks):
    @pl.when(pl.program_id(1) == 0)
    def _():
        st_ref[...] = jnp.zeros(st_ref.shape, F32)

    hk, hv = HEADS * GLA_DK, HEADS * GLA_DV

    def chunk(ci, carry):
        r0 = pl.multiple_of(ci * REC_CHUNK, REC_CHUNK)
        x = pb_ref[0, pl.ds(r0, REC_CHUNK), :]
        q = x[:, :hk] * (GLA_DK ** -0.5)
        k = x[:, hk:2 * hk]
        v = x[:, 2 * hk:2 * hk + hv]
        gate = x[:, 2 * hk + hv:2 * hk + 2 * hv]
        logits = _dot(x[:, 2 * hk + 2 * hv:].astype(BF16), wg_ref[...]) + bg_ref[...]
        g = _log_sigmoid(logits) * (1.0 / GLA_TAU)
        _rec_chunk(q, k, v, g, st_ref, p_sc, o_sc, ltri_ref, ind_ref, hmk_ref, hmv_ref, hms_ref)
        y = _head_rms_gate(o_sc[...], gate, gn_ref[...], indv_ref)
        y_ref[0, pl.ds(r0, REC_CHUNK), :] = y.astype(BF16)
        return carry

    lax.fori_loop(0, n_chunks, chunk, 0)


def _hgrn_kernel(pc_ref, lb_ref, gn_ref, ltri_ref, ind_ref, indv_ref, hmk_ref, hmv_ref, hms_ref,
                 y_ref, st_ref, p_sc, o_sc, *, n_chunks):
    @pl.when(pl.program_id(1) == 0)
    def _():
        st_ref[...] = jnp.zeros(st_ref.shape, F32)

    hk, hv = HEADS * HGRN_DK, HEADS * HGRN_DV

    def chunk(ci, carry):
        r0 = pl.multiple_of(ci * REC_CHUNK, REC_CHUNK)
        x = pc_ref[0, pl.ds(r0, REC_CHUNK), :]
        cq = x[:, :hk]
        z = x[:, hk:2 * hk]
        v = x[:, 2 * hk:2 * hk + hv]
        gate = x[:, 2 * hk + hv:]
        lb = lb_ref[...]
        f = jnp.maximum(lb, LB_FLOOR) + (1.0 - lb) * _sigmoid(z)
        g = jnp.minimum(jnp.log(f), 0.0)
        k = (1.0 - lb) * _sigmoid(-z)
        q = cq * _sigmoid(cq) * (HGRN_DK ** -0.5)
        _rec_chunk(q, k, v, g, st_ref, p_sc, o_sc, ltri_ref, ind_ref, hmk_ref, hmv_ref, hms_ref)
        y = _head_rms_gate(o_sc[...], gate, gn_ref[...], indv_ref)
        y_ref[0, pl.ds(r0, REC_CHUNK), :] = y.astype(BF16)
        return carry

    lax.fori_loop(0, n_chunks, chunk, 0)


def _recurrence(kernel, name, x, small, consts, dk, dv, ts):
    b, s, width = x.shape
    hk, hv = HEADS * dk, HEADS * dv
    ltri, ind, indv, hmk, hmv, hms = consts
    operands = (x, *small, ltri, ind, indv, hmk, hmv, hms)
    in_specs = [pl.BlockSpec((1, ts, width), lambda i, t: (i, t, 0))]
    in_specs += [_const_spec(a.shape) for a in operands[1:]]
    return pl.pallas_call(
        functools.partial(kernel, n_chunks=ts // REC_CHUNK),
        grid=(b, s // ts),
        in_specs=in_specs,
        out_specs=pl.BlockSpec((1, ts, hv), lambda i, t: (i, t, 0)),
        out_shape=jax.ShapeDtypeStruct((b, s, hv), BF16),
        scratch_shapes=[pltpu.VMEM((hv, hk), F32),
                        pltpu.VMEM((REC_BLOCK * REC_CHUNK, hk), BF16),
                        pltpu.VMEM((REC_CHUNK, hv), F32)],
        compiler_params=_params("arbitrary", "arbitrary"),
        name=name,
    )(*operands)


def _mixca_kernel(h_ref, ya_ref, yb_ref, yc_ref, wa_ref, wb_ref, wc_ref, g_ref, wq_ref, gq_ref,
                  km_ref, vm_ref, wo_ref, out_ref):
    h1 = (h_ref[0] + _dot(ya_ref[0], wa_ref[...]) + _dot(yb_ref[0], wb_ref[...])
          + _dot(yc_ref[0], wc_ref[...]))
    qc = _dot(_rms(h1, g_ref[...]).astype(BF16), wq_ref[...])
    heads = []
    for h in range(HEADS):
        qn = _rms(qc[:, CA_DIM * h:CA_DIM * (h + 1)], gq_ref[...]) * (CA_DIM ** -0.5)
        s = _dot_nt(qn.astype(BF16), km_ref[0, h])
        p = jnp.exp(s - jnp.max(s, axis=-1, keepdims=True))
        o = _dot(p.astype(BF16), vm_ref[0, h]) / jnp.sum(p, axis=-1, keepdims=True)
        heads.append(o.astype(BF16))
    out_ref[0] = h1 + _dot(jnp.concatenate(heads, axis=-1), wo_ref[...])


def _mixca(h, ya, yb, yc, wa, wb, wc, g, wq, gq, km, vm, wo, tm):
    b, s, d = h.shape
    tok = lambda last: pl.BlockSpec((1, tm, last), lambda i, t: (i, t, 0))
    mem = pl.BlockSpec((1,) + km.shape[1:], lambda i, t: (i, 0, 0, 0))
    return pl.pallas_call(
        _mixca_kernel,
        grid=(b, s // tm),
        in_specs=[tok(d), tok(ya.shape[-1]), tok(yb.shape[-1]), tok(yc.shape[-1]),
                  _const_spec(wa.shape), _const_spec(wb.shape), _const_spec(wc.shape),
                  _const_spec(g.shape), _const_spec(wq.shape), _const_spec(gq.shape),
                  mem, mem, _const_spec(wo.shape)],
        out_specs=tok(d),
        out_shape=jax.ShapeDtypeStruct((b, s, d), F32),
        compiler_params=_params("arbitrary", "arbitrary"),
        name="mix_ca",
    )(h, ya, yb, yc, wa, wb, wc, g, wq, gq, km, vm, wo)


def _ffn_kernel(h_ref, g_ref, wg_ref, wu_ref, cw_ref, cb_ref, wo_ref, out_ref, gs_sc, carry_sc, *, fc):
    tm = h_ref.shape[1]
    d_ff = wg_ref.shape[1]

    @pl.when(pl.program_id(1) == 0)
    def _():
        carry_sc[...] = jnp.zeros(carry_sc.shape, F32)

    x = h_ref[0]
    n = _rms(x, g_ref[...]).astype(BF16)
    acc = x
    for c0 in range(0, d_ff, fc):
        gate = _dot(n, wg_ref[:, c0:c0 + fc])
        gs_sc[0:8, :] = carry_sc[:, c0:c0 + fc]
        gs_sc[8:8 + tm, :] = gate
        carry_sc[:, c0:c0 + fc] = gate[tm - 8:, :]
        conv = (cb_ref[:, c0:c0 + fc] + gs_sc[6:6 + tm, :] * cw_ref[0:1, c0:c0 + fc]
                + gs_sc[7:7 + tm, :] * cw_ref[1:2, c0:c0 + fc] + gate * cw_ref[2:3, c0:c0 + fc])
        up = _dot(n, wu_ref[:, c0:c0 + fc])
        act = (conv * _sigmoid(conv) * up).astype(BF16)
        acc = acc + _dot(act, wo_ref[c0:c0 + fc, :])
    out_ref[0] = acc


def _ffn(h, g, wg, wu, cw, cb, wo, tm, fc):
    b, s, d = h.shape
    d_ff = wg.shape[1]
    tok = pl.BlockSpec((1, tm, d), lambda i, t: (i, t, 0))
    return pl.pallas_call(
        functools.partial(_ffn_kernel, fc=fc),
        grid=(b, s // tm),
        in_specs=[tok, _const_spec(g.shape), _const_spec(wg.shape), _const_spec(wu.shape),
                  _const_spec(cw.shape), _const_spec(cb.shape), _const_spec(wo.shape)],
        out_specs=tok,
        out_shape=jax.ShapeDtypeStruct((b, s, d), F32),
        scratch_shapes=[pltpu.VMEM((tm + 8, fc), F32), pltpu.VMEM((8, d_ff), F32)],
        compiler_params=_params("arbitrary", "arbitrary"),
        name="ffn",
    )(h, g, wg, wu, cw, cb, wo)


def _rope_lanes(x):
    half = MLA_ROPE // 2
    z = jnp.zeros(x.shape[:-1] + (LANES // 2 - half,), x.dtype)
    return jnp.concatenate([x[..., :half], z, x[..., half:], z], axis=-1)


def _qk_lanes(x):
    return jnp.concatenate([x[..., :MLA_NOPE], _rope_lanes(x[..., MLA_NOPE:])], axis=-1)


def _relayout_w_in(w):
    offs = np.cumsum([0, MLA_Q_LORA, MLA_KV_LORA, MLA_ROPE, HEADS * GLA_DK, HEADS * GLA_DK, HEADS * GLA_DV,
                      GLA_RANK, HEADS * GLA_DV, HEADS * HGRN_DK, HEADS * HGRN_DK, HEADS * HGRN_DV,
                      HEADS * HGRN_DV])
    seg = lambda i: w[:, offs[i]:offs[i + 1]]
    gate_lr = jnp.pad(seg(6), ((0, 0), (0, LANES - GLA_RANK)))
    return jnp.concatenate([seg(0), seg(1), _rope_lanes(seg(2)),
                            seg(3), seg(4), seg(5), seg(7), gate_lr,
                            seg(8), seg(9), seg(10), seg(11)], axis=-1)


def _pick(n, pref):
    return pref if n % pref == 0 else n


def kernel(x, mem, positions, norm_mix, w_in, mla_q_lat_norm, w_q_up, mla_kv_lat_norm, w_kv_up, mla_q_norm,
           mla_k_norm, gla_w_gate, gla_b_gate, gla_out_norm, hgrn_lb_logits, hgrn_out_norm, w_mix_out,
           norm_ca, norm_mem, w_ca_q, w_ca_kv, ca_q_norm, ca_k_norm, w_ca_out, norm_ffn, w_ffn_in,
           ffn_conv_w, ffn_conv_b, w_ffn_out):
    b, s, d = x.shape
    depth = w_in.shape[0]
    d_ff = w_ffn_out.shape[1]
    tm = _pick(s, 512)
    t_attn = _pick(s, 512)
    ts_rec = _pick(s, 512)
    fc = d_ff // 2

    gamma = jax.nn.softmax(hgrn_lb_logits.astype(F32), axis=0)
    lower_bounds = jnp.cumsum(gamma, axis=0) - gamma[0]
    inv_freq = ROPE_THETA ** (-jnp.arange(0, MLA_ROPE, 2, dtype=F32) / MLA_ROPE)
    ang = positions.astype(F32)[..., None] * inv_freq
    cos = _rope_lanes(jnp.concatenate([jnp.cos(ang), jnp.cos(ang)], axis=-1))
    sin = _rope_lanes(jnp.concatenate([-jnp.sin(ang), jnp.sin(ang)], axis=-1))

    km, vm = _memkv(mem, norm_mem, w_ca_kv.astype(BF16), ca_k_norm)
    gla_consts = _rec_consts(GLA_DK, GLA_DV)
    hgrn_consts = _rec_consts(HGRN_DK, HGRN_DV)
    row = lambda a: a.reshape(1, -1)
    n_a = HEADS * MLA_V
    n_b = HEADS * GLA_DV

    h = x
    for l in range(depth):
        w_l = _relayout_w_in(w_in[l]).astype(BF16)
        wq_l = _qk_lanes(w_q_up[l].reshape(MLA_Q_LORA, HEADS, MLA_QK)).reshape(MLA_Q_LORA, -1).astype(BF16)
        q, k, v, pb, pc = _inproj(
            h, row(norm_mix[l]), w_l, row(mla_q_lat_norm[l]), wq_l, row(mla_kv_lat_norm[l]),
            w_kv_up[l].astype(BF16), row(_qk_lanes(mla_q_norm[l])), row(_qk_lanes(mla_k_norm[l])),
            cos, sin, tm)
        ya = _mla_attn(q, k, v, t_attn, t_attn)
        wg_l = jnp.pad(gla_w_gate[l], ((0, LANES - GLA_RANK), (0, 0))).astype(BF16)
        yb = _recurrence(_gla_kernel, "gla", pb,
                         (wg_l, row(gla_b_gate[l]), row(jnp.tile(gla_out_norm[l], HEADS))),
                         gla_consts, GLA_DK, GLA_DV, ts_rec)
        yc = _recurrence(_hgrn_kernel, "hgrn", pc,
                         (row(lower_bounds[l]), row(jnp.tile(hgrn_out_norm[l], HEADS))),
                         hgrn_consts, HGRN_DK, HGRN_DV, ts_rec)
        wm = w_mix_out[l].astype(BF16)
        h = _mixca(h, ya, yb, yc, wm[:n_a], wm[n_a:n_a + n_b], wm[n_a + n_b:], row(norm_ca[l]),
                   w_ca_q[l].astype(BF16), row(ca_q_norm[l]), km[l], vm[l], w_ca_out[l].astype(BF16), tm)
        wf = w_ffn_in[l].astype(BF16)
        cw = jnp.pad(ffn_conv_w[l], ((0, 8 - ffn_conv_w.shape[1]), (0, 0)))
        h = _ffn(h, row(norm_ffn[l]), wf[:, :d_ff], wf[:, d_ff:], cw, row(ffn_conv_b[l]),
                 w_ffn_out[l].astype(BF16), tm, fc)
    return h
```

```python
import functools

import numpy as np
import jax
import jax.numpy as jnp
from jax import lax
from jax.experimental import pallas as pl
from jax.experimental.pallas import tpu as pltpu

F32 = jnp.float32
BF16 = jnp.bfloat16

EPS = 1e-6
MASK_VALUE = -1e30
LB_FLOOR = 1e-30
CHUNK = 64
ROPE_THETA = 10000.0
LOG2_E = 1.4426950408889634

HEADS = 4
MLA_Q_LORA = 256
MLA_KV_LORA = 256
MLA_NOPE = 128
MLA_ROPE = 64
MLA_V = 128
MLA_QK = MLA_NOPE + MLA_ROPE
MLA_QK_PAD = 256
GLA_DK = 32
GLA_DV = 64
GLA_RANK = 16
GLA_TAU = 16.0
HGRN_DK = 64
HGRN_DV = 64
CA_DIM = 128
N_A = 640
N_B = 896
N_C = 1024

LANES = 128
REC_CHUNK = 128
REC_BLOCK = 16
VMEM_LIMIT = 56 * 1024 * 1024

NT_DIMS = (((1,), (1,)), ((), ()))
TN_DIMS = (((0,), (0,)), ((), ()))


def _dot(a, b):
    return jnp.dot(a, b, preferred_element_type=F32)


def _dot_nt(a, b):
    return lax.dot_general(a, b, NT_DIMS, preferred_element_type=F32)


def _rms(x, gain, dim=None):
    dim = x.shape[-1] if dim is None else dim
    ms = jnp.sum(x * x, axis=-1, keepdims=True) * (1.0 / dim)
    return x * lax.rsqrt(ms + EPS) * gain


def _sigmoid(x):
    return 1.0 / (1.0 + jnp.exp(-x))


def _log_sigmoid(x):
    return jnp.minimum(x, 0.0) - jnp.log1p(jnp.exp(-jnp.abs(x)))


def _params(*sem):
    return pltpu.CompilerParams(dimension_semantics=sem, vmem_limit_bytes=VMEM_LIMIT)


def _const_spec(shape):
    nd = len(shape)
    return pl.BlockSpec(shape, lambda *_: (0,) * nd)


def _memkv_kernel(mem_ref, g_ref, w_ref, gk_ref, k_ref, v_ref):
    n = _rms(mem_ref[0], g_ref[0]).astype(BF16)
    kv = _dot(n, w_ref[0])
    for h in range(HEADS):
        base = 2 * CA_DIM * h
        k_ref[0, 0, h] = _rms(kv[:, base:base + CA_DIM], gk_ref[0]).astype(BF16)
        v_ref[0, 0, h] = kv[:, base + CA_DIM:base + 2 * CA_DIM].astype(BF16)


def _memkv(mem, norm_mem, w_ca_kv, ca_k_norm):
    depth, d = norm_mem.shape
    b, m, _ = mem.shape
    out = jax.ShapeDtypeStruct((depth, b, HEADS, m, CA_DIM), BF16)
    spec_o = pl.BlockSpec((1, 1, HEADS, m, CA_DIM), lambda l, i: (l, i, 0, 0, 0))
    return pl.pallas_call(
        _memkv_kernel,
        grid=(depth, b),
        in_specs=[
            pl.BlockSpec((1, m, d), lambda l, i: (i, 0, 0)),
            pl.BlockSpec((1, 1, d), lambda l, i: (l, 0, 0)),
            pl.BlockSpec((1, d, w_ca_kv.shape[-1]), lambda l, i: (l, 0, 0)),
            pl.BlockSpec((1, 1, CA_DIM), lambda l, i: (l, 0, 0)),
        ],
        out_specs=[spec_o, spec_o],
        out_shape=[out, out],
        compiler_params=_params("arbitrary", "arbitrary"),
        name="memkv",
    )(mem, norm_mem.reshape(depth, 1, d), w_ca_kv, ca_k_norm.reshape(depth, 1, CA_DIM))


def _rope(x, cos, sin):
    return x * cos + pltpu.roll(x, 64, 1) * sin


def _inproj_kernel(h_ref, g_ref, w_ref, gql_ref, wq_ref, gkvl_ref, wkv_ref, gq_ref, gk_ref,
                   cos_ref, sin_ref, q_ref, k_ref, v_ref, pb_ref, pc_ref):
    n = _rms(h_ref[0], g_ref[...]).astype(BF16)
    pa = _dot(n, w_ref[:, :N_A])
    pb_ref[0] = _dot(n, w_ref[:, N_A:N_A + N_B])
    pc_ref[0] = _dot(n, w_ref[:, N_A + N_B:])
    cos = cos_ref[0]
    sin = sin_ref[0]
    qf = _dot(_rms(pa[:, :MLA_Q_LORA], gql_ref[...]).astype(BF16), wq_ref[...])
    kvf = _dot(_rms(pa[:, MLA_Q_LORA:MLA_Q_LORA + MLA_KV_LORA], gkvl_ref[...]).astype(BF16),
               wkv_ref[...])
    kr = pa[:, MLA_Q_LORA + MLA_KV_LORA:]
    kr_ss = jnp.sum(kr * kr, axis=-1, keepdims=True)
    gq = gq_ref[...]
    gk = gk_ref[...]
    scale = MLA_QK ** -0.5 * LOG2_E
    for h in range(HEADS):
        base = MLA_QK_PAD * h
        qn = _rms(qf[:, base:base + MLA_QK_PAD], gq, MLA_QK) * scale
        q_ref[0, h, :, :MLA_NOPE] = qn[:, :MLA_NOPE].astype(BF16)
        q_ref[0, h, :, MLA_NOPE:] = _rope(qn[:, MLA_NOPE:], cos, sin).astype(BF16)
        kn = kvf[:, base:base + MLA_NOPE]
        ms = (jnp.sum(kn * kn, axis=-1, keepdims=True) + kr_ss) * (1.0 / MLA_QK)
        rs = lax.rsqrt(ms + EPS)
        k_ref[0, h, :, :MLA_NOPE] = (kn * rs * gk[:, :MLA_NOPE]).astype(BF16)
        k_ref[0, h, :, MLA_NOPE:] = _rope(kr * rs * gk[:, MLA_NOPE:], cos, sin).astype(BF16)
        v_ref[0, h] = kvf[:, base + MLA_NOPE:base + MLA_NOPE + MLA_V].astype(BF16)


def _inproj(h, g, w, gql, wq, gkvl, wkv, gq, gk, cos, sin, tm):
    b, s, d = h.shape
    grid = (b, s // tm)
    tok = lambda last: pl.BlockSpec((1, tm, last), lambda i, t: (i, t, 0))
    head = lambda last: pl.BlockSpec((1, HEADS, tm, last), lambda i, t: (i, 0, t, 0))
    return pl.pallas_call(
        _inproj_kernel,
        grid=grid,
        in_specs=[tok(d), _const_spec(g.shape), _const_spec(w.shape), _const_spec(gql.shape),
                  _const_spec(wq.shape), _const_spec(gkvl.shape), _const_spec(wkv.shape),
                  _const_spec(gq.shape), _const_spec(gk.shape), tok(LANES), tok(LANES)],
        out_specs=[head(MLA_QK_PAD), head(MLA_QK_PAD), head(MLA_V), tok(N_B), tok(N_C)],
        out_shape=[jax.ShapeDtypeStruct((b, HEADS, s, MLA_QK_PAD), BF16),
                   jax.ShapeDtypeStruct((b, HEADS, s, MLA_QK_PAD), BF16),
                   jax.ShapeDtypeStruct((b, HEADS, s, MLA_V), BF16),
                   jax.ShapeDtypeStruct((b, s, N_B), F32),
                   jax.ShapeDtypeStruct((b, s, N_C), F32)],
        compiler_params=_params("arbitrary", "arbitrary"),
        name="inproj",
    )(h, g, w, gql, wq, gkvl, wkv, gq, gk, cos, sin)


def _mla_attn_kernel(q_ref, k_ref, v_ref, o_ref, m_sc, l_sc, acc_sc, sa_sc, sb_sc, *, t):
    qi = pl.program_id(2)
    q = q_ref[0, 0]
    m_sc[...] = jnp.full(m_sc.shape, MASK_VALUE, F32)
    l_sc[...] = jnp.zeros(l_sc.shape, F32)
    acc_sc[...] = jnp.zeros(acc_sc.shape, F32)

    def scores(j):
        return _dot_nt(q, k_ref[0, 0, pl.ds(pl.multiple_of(j * t, t), t), :])

    def update(s, j):
        m_prev = m_sc[...]
        m_new = jnp.maximum(m_prev, jnp.max(s, axis=-1, keepdims=True))
        alpha = jnp.exp2(m_prev - m_new)
        p = jnp.exp2(s - pltpu.repeat(m_new, t // LANES, axis=1))
        l_sc[...] = alpha * l_sc[...] + jnp.sum(p, axis=-1, keepdims=True)
        v_tile = v_ref[0, 0, pl.ds(pl.multiple_of(j * t, t), t), :]
        acc_sc[...] = alpha * acc_sc[...] + _dot(p.astype(BF16), v_tile)
        m_sc[...] = m_new

    def update_diag(s):
        qc = lax.broadcasted_iota(jnp.int32, (t, t), 0) // CHUNK
        kc = lax.broadcasted_iota(jnp.int32, (t, t), 1) // CHUNK
        update(jnp.where(kc <= qc, s, MASK_VALUE), qi)

    sa_sc[...] = scores(0)

    def pair(jj, carry):
        j = 2 * jj
        sb_sc[...] = scores(j + 1)
        update(sa_sc[...], j)
        sa_sc[...] = scores(j + 2)
        update(sb_sc[...], j + 1)
        return carry

    lax.fori_loop(0, qi // 2, pair, 0)

    @pl.when(qi % 2 == 1)
    def _():
        sb_sc[...] = scores(qi)
        update(sa_sc[...], qi - 1)
        update_diag(sb_sc[...])

    @pl.when(qi % 2 == 0)
    def _():
        update_diag(sa_sc[...])

    o_ref[0] = (acc_sc[...] / l_sc[...]).astype(BF16)


def _mla_attn(q, k, v, t):
    b, nh, s, dq = q.shape
    dv = v.shape[-1]
    return pl.pallas_call(
        functools.partial(_mla_attn_kernel, t=t),
        grid=(b, nh, s // t),
        in_specs=[pl.BlockSpec((1, 1, t, dq), lambda i, h, j: (i, h, j, 0)),
                  pl.BlockSpec((1, 1, s, dq), lambda i, h, j: (i, h, 0, 0)),
                  pl.BlockSpec((1, 1, s, dv), lambda i, h, j: (i, h, 0, 0))],
        out_specs=pl.BlockSpec((1, t, dv), lambda i, h, j: (i, j, h)),
        out_shape=jax.ShapeDtypeStruct((b, s, nh * dv), BF16),
        scratch_shapes=[pltpu.VMEM((t, LANES), F32), pltpu.VMEM((t, LANES), F32), pltpu.VMEM((t, dv), F32),
                        pltpu.VMEM((t, t), F32), pltpu.VMEM((t, t), F32)],
        compiler_params=_params("arbitrary", "arbitrary", "arbitrary"),
        name="mla_attn",
    )(q, k, v)


def _rec_consts(dk, dv):
    hk, hv = HEADS * dk, HEADS * dv
    c, blk = REC_CHUNK, REC_BLOCK
    head_k = np.arange(hk) // dk
    head_v = np.arange(hv) // dv
    head_r = np.repeat(np.arange(HEADS), blk)
    ltri = (np.arange(c)[:, None] >= np.arange(c)[None, :])
    ind = head_k[:, None] == head_v[None, :]
    indv = (head_v[:, None] == head_v[None, :]) / dv
    hm_k = head_r[:, None] == head_k[None, :]
    hm_v = head_r[:, None] == head_v[None, :]
    return (jnp.asarray(ltri, BF16), jnp.asarray(ind, BF16), jnp.asarray(indv, BF16),
            jnp.asarray(hm_k, F32), jnp.asarray(hm_v, F32), jnp.asarray(ind.T, F32))


def _rec_chunk(q, k, v, g, st_ref, p_sc, o_sc, ltri_ref, ind_ref, hmk_ref, hmv_ref, hms_ref):
    c, hk = q.shape
    hv = v.shape[1]
    blk = REC_BLOCK
    nb = c // blk
    g_hi = g.astype(BF16)
    g_lo = (g - g_hi.astype(F32)).astype(BF16)
    ltri = ltri_ref[...]
    b = _dot(ltri, g_hi) + _dot(ltri, g_lo)

    b3 = b.reshape(nb, blk, hk)
    q3 = q.reshape(nb, blk, hk)
    k3 = k.reshape(nb, blk, hk)
    v3 = v.reshape(nb, blk, hv)
    row = lax.broadcasted_iota(jnp.int32, (nb, blk, hk), 1)
    for jj in range(blk):
        e = jnp.exp(jnp.minimum(b3 - b3[:, jj:jj + 1, :], 0.0))
        p = jnp.where(row >= jj, q3 * k3[:, jj:jj + 1, :] * e, 0.0)
        p_sc[jj * c:(jj + 1) * c, :] = p.reshape(c, hk).astype(BF16)
    r = _dot(p_sc[...], ind_ref[...])
    o = jnp.zeros((nb, blk, hv), F32)
    for jj in range(blk):
        o = o + r[jj * c:(jj + 1) * c, :].reshape(nb, blk, hv) * v3[:, jj:jj + 1, :]

    st = st_ref[...]
    o = o.reshape(c, hv) + _dot_nt((q * jnp.exp(b)).astype(BF16), st.astype(BF16))
    o_sc[...] = o

    hmk = hmk_ref[...]
    hmv = hmv_ref[...]
    for jb in range(nb - 1):
        lo, r0 = blk * jb, blk * (jb + 1)
        b_end = b[r0 - 1:r0, :]
        qj = (q[r0:, :] * jnp.exp(b[r0:, :] - b_end)).astype(BF16)
        kj = k[lo:r0, :] * jnp.exp(b_end - b[lo:r0, :])
        kbd = (jnp.concatenate([kj] * HEADS, axis=0) * hmk).astype(BF16)
        vbd = (jnp.concatenate([v[lo:r0, :]] * HEADS, axis=0) * hmv).astype(BF16)
        a = _dot_nt(qj, kbd)
        o_sc[r0:, :] += _dot(a.astype(BF16), vbd)

    b_last = b[c - 1:c, :]
    kh = (k * jnp.exp(b_last - b)).astype(BF16)
    upd = lax.dot_general(v.astype(BF16), kh, TN_DIMS, preferred_element_type=F32)
    st_ref[...] = st * jnp.exp(b_last) + upd * hms_ref[...]


def _head_rms_gate(o, gate, gain, indv_ref):
    ms = _dot((o * o).astype(BF16), indv_ref[...])
    return o * lax.rsqrt(ms + EPS) * gain * (gate * _sigmoid(gate))


def _gla_kernel(pb_ref, wg_ref, bg_ref, gn_ref, ltri_ref, ind_ref, indv_ref, hmk_ref, hmv_ref, hms_ref,
                y_ref, st_ref, p_sc, o_sc, *, n_chunks):
    @pl.when(pl.program_id(1) == 0)
    def _():
        st_ref[...] = jnp.zeros(st_ref.shape, F32)

    hk, hv = HEADS * GLA_DK, HEADS * GLA_DV

    for ci in range(n_chunks):
        r0 = ci * REC_CHUNK
        x = pb_ref[0, pl.ds(r0, REC_CHUNK), :]
        q = x[:, :hk] * (GLA_DK ** -0.5)
        k = x[:, hk:2 * hk]
        v = x[:, 2 * hk:2 * hk + hv]
        gate = x[:, 2 * hk + hv:2 * hk + 2 * hv]
        logits = _dot(x[:, 2 * hk + 2 * hv:].astype(BF16), wg_ref[...]) + bg_ref[...]
        g = _log_sigmoid(logits) * (1.0 / GLA_TAU)
        _rec_chunk(q, k, v, g, st_ref, p_sc.at[ci], o_sc.at[ci], ltri_ref, ind_ref, hmk_ref, hmv_ref,
                   hms_ref)
        y = _head_rms_gate(o_sc[ci], gate, gn_ref[...], indv_ref)
        y_ref[0, pl.ds(r0, REC_CHUNK), :] = y.astype(BF16)


def _hgrn_kernel(pc_ref, lb_ref, gn_ref, ltri_ref, ind_ref, indv_ref, hmk_ref, hmv_ref, hms_ref,
                 y_ref, st_ref, p_sc, o_sc, *, n_chunks):
    @pl.when(pl.program_id(1) == 0)
    def _():
        st_ref[...] = jnp.zeros(st_ref.shape, F32)

    hk, hv = HEADS * HGRN_DK, HEADS * HGRN_DV

    for ci in range(n_chunks):
        r0 = ci * REC_CHUNK
        x = pc_ref[0, pl.ds(r0, REC_CHUNK), :]
        cq = x[:, :hk]
        z = x[:, hk:2 * hk]
        v = x[:, 2 * hk:2 * hk + hv]
        gate = x[:, 2 * hk + hv:]
        lb = lb_ref[...]
        f = jnp.maximum(lb, LB_FLOOR) + (1.0 - lb) * _sigmoid(z)
        g = jnp.minimum(jnp.log(f), 0.0)
        k = (1.0 - lb) * _sigmoid(-z)
        q = cq * _sigmoid(cq) * (HGRN_DK ** -0.5)
        _rec_chunk(q, k, v, g, st_ref, p_sc.at[ci], o_sc.at[ci], ltri_ref, ind_ref, hmk_ref, hmv_ref,
                   hms_ref)
        y = _head_rms_gate(o_sc[ci], gate, gn_ref[...], indv_ref)
        y_ref[0, pl.ds(r0, REC_CHUNK), :] = y.astype(BF16)


def _recurrence(kernel, name, x, small, consts, dk, dv, ts):
    b, s, width = x.shape
    hk, hv = HEADS * dk, HEADS * dv
    ltri, ind, indv, hmk, hmv, hms = consts
    operands = (x, *small, ltri, ind, indv, hmk, hmv, hms)
    in_specs = [pl.BlockSpec((1, ts, width), lambda i, t: (i, t, 0))]
    in_specs += [_const_spec(a.shape) for a in operands[1:]]
    return pl.pallas_call(
        functools.partial(kernel, n_chunks=ts // REC_CHUNK),
        grid=(b, s // ts),
        in_specs=in_specs,
        out_specs=pl.BlockSpec((1, ts, hv), lambda i, t: (i, t, 0)),
        out_shape=jax.ShapeDtypeStruct((b, s, hv), BF16),
        scratch_shapes=[pltpu.VMEM((hv, hk), F32),
                        pltpu.VMEM((ts // REC_CHUNK, REC_BLOCK * REC_CHUNK, hk), BF16),
                        pltpu.VMEM((ts // REC_CHUNK, REC_CHUNK, hv), F32)],
        compiler_params=_params("arbitrary", "arbitrary"),
        name=name,
    )(*operands)


def _mixca_kernel(h_ref, ya_ref, yb_ref, yc_ref, wa_ref, wb_ref, wc_ref, g_ref, wq_ref, gq_ref,
                  km_ref, vm_ref, wo_ref, out_ref):
    h1 = (h_ref[0] + _dot(ya_ref[0], wa_ref[...]) + _dot(yb_ref[0], wb_ref[...])
          + _dot(yc_ref[0], wc_ref[...]))
    qc = _dot(_rms(h1, g_ref[...]).astype(BF16), wq_ref[...])
    heads = []
    for h in range(HEADS):
        qn = _rms(qc[:, CA_DIM * h:CA_DIM * (h + 1)], gq_ref[...]) * (CA_DIM ** -0.5)
        s = _dot_nt(qn.astype(BF16), km_ref[0, h])
        p = jnp.exp(s - jnp.max(s, axis=-1, keepdims=True))
        o = _dot(p.astype(BF16), vm_ref[0, h]) / jnp.sum(p, axis=-1, keepdims=True)
        heads.append(o.astype(BF16))
    out_ref[0] = h1 + _dot(jnp.concatenate(heads, axis=-1), wo_ref[...])


def _mixca(h, ya, yb, yc, wa, wb, wc, g, wq, gq, km, vm, wo, tm):
    b, s, d = h.shape
    tok = lambda last: pl.BlockSpec((1, tm, last), lambda i, t: (i, t, 0))
    mem = pl.BlockSpec((1,) + km.shape[1:], lambda i, t: (i, 0, 0, 0))
    return pl.pallas_call(
        _mixca_kernel,
        grid=(b, s // tm),
        in_specs=[tok(d), tok(ya.shape[-1]), tok(yb.shape[-1]), tok(yc.shape[-1]),
                  _const_spec(wa.shape), _const_spec(wb.shape), _const_spec(wc.shape),
                  _const_spec(g.shape), _const_spec(wq.shape), _const_spec(gq.shape),
                  mem, mem, _const_spec(wo.shape)],
        out_specs=tok(d),
        out_shape=jax.ShapeDtypeStruct((b, s, d), F32),
        compiler_params=_params("arbitrary", "arbitrary"),
        name="mix_ca",
    )(h, ya, yb, yc, wa, wb, wc, g, wq, gq, km, vm, wo)


def _ffn_kernel(h_ref, g_ref, wg_ref, wu_ref, cw_ref, cb_ref, wo_ref, out_ref, gs_sc, carry_sc, *, fc):
    tm = h_ref.shape[1]
    d_ff = wg_ref.shape[1]

    @pl.when(pl.program_id(1) == 0)
    def _():
        carry_sc[...] = jnp.zeros(carry_sc.shape, F32)

    x = h_ref[0]
    n = _rms(x, g_ref[...]).astype(BF16)
    acc = x
    for c0 in range(0, d_ff, fc):
        gate = _dot(n, wg_ref[:, c0:c0 + fc])
        gs_sc[0:8, :] = carry_sc[:, c0:c0 + fc]
        gs_sc[8:8 + tm, :] = gate
        carry_sc[:, c0:c0 + fc] = gate[tm - 8:, :]
        conv = (cb_ref[:, c0:c0 + fc] + gs_sc[6:6 + tm, :] * cw_ref[0:1, c0:c0 + fc]
                + gs_sc[7:7 + tm, :] * cw_ref[1:2, c0:c0 + fc] + gate * cw_ref[2:3, c0:c0 + fc])
        up = _dot(n, wu_ref[:, c0:c0 + fc])
        act = (conv * _sigmoid(conv) * up).astype(BF16)
        acc = acc + _dot(act, wo_ref[c0:c0 + fc, :])
    out_ref[0] = acc


def _ffn(h, g, wg, wu, cw, cb, wo, tm, fc):
    b, s, d = h.shape
    d_ff = wg.shape[1]
    tok = pl.BlockSpec((1, tm, d), lambda i, t: (i, t, 0))
    return pl.pallas_call(
        functools.partial(_ffn_kernel, fc=fc),
        grid=(b, s // tm),
        in_specs=[tok, _const_spec(g.shape), _const_spec(wg.shape), _const_spec(wu.shape),
                  _const_spec(cw.shape), _const_spec(cb.shape), _const_spec(wo.shape)],
        out_specs=tok,
        out_shape=jax.ShapeDtypeStruct((b, s, d), F32),
        scratch_shapes=[pltpu.VMEM((tm + 8, fc), F32), pltpu.VMEM((8, d_ff), F32)],
        compiler_params=_params("arbitrary", "arbitrary"),
        name="ffn",
    )(h, g, wg, wu, cw, cb, wo)


def _rope_lanes(x):
    half = MLA_ROPE // 2
    z = jnp.zeros(x.shape[:-1] + (LANES // 2 - half,), x.dtype)
    return jnp.concatenate([x[..., :half], z, x[..., half:], z], axis=-1)


def _qk_lanes(x):
    return jnp.concatenate([x[..., :MLA_NOPE], _rope_lanes(x[..., MLA_NOPE:])], axis=-1)


def _relayout_w_in(w):
    offs = np.cumsum([0, MLA_Q_LORA, MLA_KV_LORA, MLA_ROPE, HEADS * GLA_DK, HEADS * GLA_DK, HEADS * GLA_DV,
                      GLA_RANK, HEADS * GLA_DV, HEADS * HGRN_DK, HEADS * HGRN_DK, HEADS * HGRN_DV,
                      HEADS * HGRN_DV])
    seg = lambda i: w[:, offs[i]:offs[i + 1]]
    gate_lr = jnp.pad(seg(6), ((0, 0), (0, LANES - GLA_RANK)))
    return jnp.concatenate([seg(0), seg(1), _rope_lanes(seg(2)),
                            seg(3), seg(4), seg(5), seg(7), gate_lr,
                            seg(8), seg(9), seg(10), seg(11)], axis=-1)


def _pick(n, pref):
    return pref if n % pref == 0 else n


def kernel(x, mem, positions, norm_mix, w_in, mla_q_lat_norm, w_q_up, mla_kv_lat_norm, w_kv_up, mla_q_norm,
           mla_k_norm, gla_w_gate, gla_b_gate, gla_out_norm, hgrn_lb_logits, hgrn_out_norm, w_mix_out,
           norm_ca, norm_mem, w_ca_q, w_ca_kv, ca_q_norm, ca_k_norm, w_ca_out, norm_ffn, w_ffn_in,
           ffn_conv_w, ffn_conv_b, w_ffn_out):
    b, s, d = x.shape
    depth = w_in.shape[0]
    d_ff = w_ffn_out.shape[1]
    tm = _pick(s, 512)
    t_attn = _pick(s, 512)
    ts_rec = _pick(s, 512)
    fc = d_ff // 2

    gamma = jax.nn.softmax(hgrn_lb_logits.astype(F32), axis=0)
    lower_bounds = jnp.cumsum(gamma, axis=0) - gamma[0]
    inv_freq = ROPE_THETA ** (-jnp.arange(0, MLA_ROPE, 2, dtype=F32) / MLA_ROPE)
    ang = positions.astype(F32)[..., None] * inv_freq
    cos = _rope_lanes(jnp.concatenate([jnp.cos(ang), jnp.cos(ang)], axis=-1))
    sin = _rope_lanes(jnp.concatenate([-jnp.sin(ang), jnp.sin(ang)], axis=-1))

    km, vm = _memkv(mem, norm_mem, w_ca_kv.astype(BF16), ca_k_norm)
    gla_consts = _rec_consts(GLA_DK, GLA_DV)
    hgrn_consts = _rec_consts(HGRN_DK, HGRN_DV)
    row = lambda a: a.reshape(1, -1)
    n_a = HEADS * MLA_V
    n_b = HEADS * GLA_DV

    h = x
    for l in range(depth):
        w_l = _relayout_w_in(w_in[l]).astype(BF16)
        wq_l = _qk_lanes(w_q_up[l].reshape(MLA_Q_LORA, HEADS, MLA_QK)).reshape(MLA_Q_LORA, -1).astype(BF16)
        q, k, v, pb, pc = _inproj(
            h, row(norm_mix[l]), w_l, row(mla_q_lat_norm[l]), wq_l, row(mla_kv_lat_norm[l]),
            w_kv_up[l].astype(BF16), row(_qk_lanes(mla_q_norm[l])), row(_qk_lanes(mla_k_norm[l])),
            cos, sin, tm)
        ya = _mla_attn(q, k, v, t_attn)
        wg_l = jnp.pad(gla_w_gate[l], ((0, LANES - GLA_RANK), (0, 0))).astype(BF16)
        yb = _recurrence(_gla_kernel, "gla", pb,
                         (wg_l, row(gla_b_gate[l]), row(jnp.tile(gla_out_norm[l], HEADS))),
                         gla_consts, GLA_DK, GLA_DV, ts_rec)
        yc = _recurrence(_hgrn_kernel, "hgrn", pc,
                         (row(lower_bounds[l]), row(jnp.tile(hgrn_out_norm[l], HEADS))),
                         hgrn_consts, HGRN_DK, HGRN_DV, ts_rec)
        wm = w_mix_out[l].astype(BF16)
        h = _mixca(h, ya, yb, yc, wm[:n_a], wm[n_a:n_a + n_b], wm[n_a + n_b:], row(norm_ca[l]),
                   w_ca_q[l].astype(BF16), row(ca_q_norm[l]), km[l], vm[l], w_ca_out[l].astype(BF16), tm)
        wf = w_ffn_in[l].astype(BF16)
        cw = jnp.pad(ffn_conv_w[l], ((0, 8 - ffn_conv_w.shape[1]), (0, 0)))
        h = _ffn(h, row(norm_ffn[l]), wf[:, :d_ff], wf[:, d_ff:], cw, row(ffn_conv_b[l]),
                 w_ffn_out[l].astype(BF16), tm, fc)
    return h
```

```python
import functools

import numpy as np
import jax
import jax.numpy as jnp
from jax import lax
from jax.experimental import pallas as pl
from jax.experimental.pallas import tpu as pltpu

F32 = jnp.float32
BF16 = jnp.bfloat16

EPS = 1e-6
MASK_VALUE = -1e30
LB_FLOOR = 1e-30
CHUNK = 64
ROPE_THETA = 10000.0
LOG2_E = 1.4426950408889634

HEADS = 4
MLA_Q_LORA = 256
MLA_KV_LORA = 256
MLA_NOPE = 128
MLA_ROPE = 64
MLA_V = 128
MLA_QK = MLA_NOPE + MLA_ROPE
MLA_QK_PAD = 256
GLA_DK = 32
GLA_DV = 64
GLA_RANK = 16
GLA_TAU = 16.0
HGRN_DK = 64
HGRN_DV = 64
CA_DIM = 128
MLA_Q_COLS = 384
N_A = 768
N_B = 896
N_C = 1024

LANES = 128
REC_CHUNK = 128
REC_BLOCK = 16
VMEM_LIMIT = 56 * 1024 * 1024

NT_DIMS = (((1,), (1,)), ((), ()))
TN_DIMS = (((0,), (0,)), ((), ()))


def _dot(a, b):
    return jnp.dot(a, b, preferred_element_type=F32)


def _dot_nt(a, b):
    return lax.dot_general(a, b, NT_DIMS, preferred_element_type=F32)


def _rms(x, gain, dim=None):
    dim = x.shape[-1] if dim is None else dim
    ms = jnp.sum(x * x, axis=-1, keepdims=True) * (1.0 / dim)
    return x * lax.rsqrt(ms + EPS) * gain


def _sigmoid(x):
    return 1.0 / (1.0 + jnp.exp(-x))


def _log_sigmoid(x):
    return jnp.minimum(x, 0.0) - jnp.log1p(jnp.exp(-jnp.abs(x)))


def _params(*sem):
    return pltpu.CompilerParams(dimension_semantics=sem, vmem_limit_bytes=VMEM_LIMIT)


def _const_spec(shape):
    nd = len(shape)
    return pl.BlockSpec(shape, lambda *_: (0,) * nd)


def _memkv_kernel(mem_ref, g_ref, w_ref, gk_ref, k_ref, v_ref):
    n = _rms(mem_ref[0], g_ref[0]).astype(BF16)
    kv = _dot(n, w_ref[0])
    for h in range(HEADS):
        base = 2 * CA_DIM * h
        k_ref[0, 0, h] = _rms(kv[:, base:base + CA_DIM], gk_ref[0]).astype(BF16)
        v_ref[0, 0, h] = kv[:, base + CA_DIM:base + 2 * CA_DIM].astype(BF16)


def _memkv(mem, norm_mem, w_ca_kv, ca_k_norm):
    depth, d = norm_mem.shape
    b, m, _ = mem.shape
    out = jax.ShapeDtypeStruct((depth, b, HEADS, m, CA_DIM), BF16)
    spec_o = pl.BlockSpec((1, 1, HEADS, m, CA_DIM), lambda l, i: (l, i, 0, 0, 0))
    return pl.pallas_call(
        _memkv_kernel,
        grid=(depth, b),
        in_specs=[
            pl.BlockSpec((1, m, d), lambda l, i: (i, 0, 0)),
            pl.BlockSpec((1, 1, d), lambda l, i: (l, 0, 0)),
            pl.BlockSpec((1, d, w_ca_kv.shape[-1]), lambda l, i: (l, 0, 0)),
            pl.BlockSpec((1, 1, CA_DIM), lambda l, i: (l, 0, 0)),
        ],
        out_specs=[spec_o, spec_o],
        out_shape=[out, out],
        compiler_params=_params("arbitrary", "arbitrary"),
        name="memkv",
    )(mem, norm_mem.reshape(depth, 1, d), w_ca_kv, ca_k_norm.reshape(depth, 1, CA_DIM))


def _inproj_kernel(h_ref, g_ref, w_ref, gql_ref, wq_ref, gkvl_ref, wkv_ref, gq_ref, gk_ref,
                   cos_ref, sin_ref, ones_ref, bd_ref, q_ref, k_ref, v_ref, pb_ref, pc_ref):
    n = _rms(h_ref[0], g_ref[...]).astype(BF16)
    pa = _dot(n, w_ref[:, :N_A])
    pb_ref[0] = _dot(n, w_ref[:, N_A:N_A + N_B])
    pc_ref[0] = _dot(n, w_ref[:, N_A + N_B:])

    def latent(x, gain):
        ms = _dot((x * x).astype(BF16), ones_ref[...]) * (1.0 / x.shape[1])
        return (x * lax.rsqrt(ms + EPS) * gain).astype(BF16)

    qf = _dot(latent(pa[:, :MLA_Q_LORA], gql_ref[...]), wq_ref[...])
    kvf = _dot(latent(pa[:, MLA_Q_LORA:MLA_Q_LORA + MLA_KV_LORA], gkvl_ref[...]), wkv_ref[...])
    kr = pa[:, N_A - 2 * LANES:N_A - LANES]
    kr_sw = pa[:, N_A - LANES:]
    cos = cos_ref[0]
    sin = sin_ref[0]
    gq = gq_ref[...]
    gk = gk_ref[...]
    scale = MLA_QK ** -0.5 * LOG2_E
    gq_nope = gq[:, :LANES] * scale
    q_cos = gq[:, LANES:2 * LANES] * cos * scale
    q_sin = gq[:, 2 * LANES:] * sin * scale
    k_rot = kr * (gk[:, LANES:2 * LANES] * cos) + kr_sw * (gk[:, 2 * LANES:] * sin)
    kr_sq = kr * kr
    q_sq = jnp.concatenate(
        [qf[:, MLA_Q_COLS * h:MLA_Q_COLS * h + LANES] ** 2
         + qf[:, MLA_Q_COLS * h + LANES:MLA_Q_COLS * h + 2 * LANES] ** 2 for h in range(HEADS)], axis=1)
    k_sq = jnp.concatenate(
        [kvf[:, 2 * LANES * h:2 * LANES * h + LANES] ** 2 + kr_sq for h in range(HEADS)], axis=1)
    q_rs = lax.rsqrt(_dot(q_sq.astype(BF16), bd_ref[...]) * (1.0 / MLA_QK) + EPS)
    k_rs = lax.rsqrt(_dot(k_sq.astype(BF16), bd_ref[...]) * (1.0 / MLA_QK) + EPS)
    for h in range(HEADS):
        qb = MLA_Q_COLS * h
        rs = q_rs[:, LANES * h:LANES * (h + 1)]
        q_ref[0, h, :, :MLA_NOPE] = (qf[:, qb:qb + LANES] * rs * gq_nope).astype(BF16)
        q_ref[0, h, :, MLA_NOPE:] = ((qf[:, qb + LANES:qb + 2 * LANES] * q_cos
                                      + qf[:, qb + 2 * LANES:qb + 3 * LANES] * q_sin) * rs).astype(BF16)
        kb = 2 * LANES * h
        rs = k_rs[:, LANES * h:LANES * (h + 1)]
        k_ref[0, h, :, :MLA_NOPE] = (kvf[:, kb:kb + LANES] * rs * gk[:, :LANES]).astype(BF16)
        k_ref[0, h, :, MLA_NOPE:] = (k_rot * rs).astype(BF16)
        v_ref[0, h] = kvf[:, kb + LANES:kb + 2 * LANES].astype(BF16)


def _inproj(h, g, w, gql, wq, gkvl, wkv, gq, gk, cos, sin, tm):
    ones = jnp.ones((MLA_Q_LORA, MLA_Q_LORA), BF16)
    head_of = np.arange(HEADS * LANES) // LANES
    bd = jnp.asarray(head_of[:, None] == head_of[None, :], BF16)
    b, s, d = h.shape
    grid = (b, s // tm)
    tok = lambda last: pl.BlockSpec((1, tm, last), lambda i, t: (i, t, 0))
    head = lambda last: pl.BlockSpec((1, HEADS, tm, last), lambda i, t: (i, 0, t, 0))
    return pl.pallas_call(
        _inproj_kernel,
        grid=grid,
        in_specs=[tok(d), _const_spec(g.shape), _const_spec(w.shape), _const_spec(gql.shape),
                  _const_spec(wq.shape), _const_spec(gkvl.shape), _const_spec(wkv.shape),
                  _const_spec(gq.shape), _const_spec(gk.shape), tok(LANES), tok(LANES),
                  _const_spec(ones.shape), _const_spec(bd.shape)],
        out_specs=[head(MLA_QK_PAD), head(MLA_QK_PAD), head(MLA_V), tok(N_B), tok(N_C)],
        out_shape=[jax.ShapeDtypeStruct((b, HEADS, s, MLA_QK_PAD), BF16),
                   jax.ShapeDtypeStruct((b, HEADS, s, MLA_QK_PAD), BF16),
                   jax.ShapeDtypeStruct((b, HEADS, s, MLA_V), BF16),
                   jax.ShapeDtypeStruct((b, s, N_B), F32),
                   jax.ShapeDtypeStruct((b, s, N_C), F32)],
        compiler_params=_params("arbitrary", "arbitrary"),
        name="inproj",
    )(h, g, w, gql, wq, gkvl, wkv, gq, gk, cos, sin, ones, bd)


def _mla_attn_kernel(q_ref, k_ref, v_ref, o_ref, m_sc, l_sc, acc_sc, sa_sc, sb_sc, *, t):
    qi = pl.program_id(2)
    q = q_ref[0, 0]
    m_sc[...] = jnp.full(m_sc.shape, MASK_VALUE, F32)
    l_sc[...] = jnp.zeros(l_sc.shape, F32)
    acc_sc[...] = jnp.zeros(acc_sc.shape, F32)

    def scores(j):
        return _dot_nt(q, k_ref[0, 0, pl.ds(pl.multiple_of(j * t, t), t), :])

    def update(s, j):
        m_prev = m_sc[...]
        m_new = jnp.maximum(m_prev, jnp.max(s, axis=-1, keepdims=True))
        alpha = jnp.exp2(m_prev - m_new)
        p = jnp.exp2(s - jnp.concatenate([m_new] * (t // LANES), axis=1))
        l_sc[...] = alpha * l_sc[...] + jnp.sum(p, axis=-1, keepdims=True)
        v_tile = v_ref[0, 0, pl.ds(pl.multiple_of(j * t, t), t), :]
        acc_sc[...] = alpha * acc_sc[...] + _dot(p.astype(BF16), v_tile)
        m_sc[...] = m_new

    def update_diag(s):
        qc = lax.broadcasted_iota(jnp.int32, (t, t), 0) // CHUNK
        kc = lax.broadcasted_iota(jnp.int32, (t, t), 1) // CHUNK
        update(jnp.where(kc <= qc, s, MASK_VALUE), qi)

    sa_sc[...] = scores(0)

    def pair(jj, carry):
        j = 2 * jj
        sb_sc[...] = scores(j + 1)
        update(sa_sc[...], j)
        sa_sc[...] = scores(j + 2)
        update(sb_sc[...], j + 1)
        return carry

    lax.fori_loop(0, qi // 2, pair, 0)

    @pl.when(qi % 2 == 1)
    def _():
        sb_sc[...] = scores(qi)
        update(sa_sc[...], qi - 1)
        update_diag(sb_sc[...])

    @pl.when(qi % 2 == 0)
    def _():
        update_diag(sa_sc[...])

    o_ref[0] = (acc_sc[...] / l_sc[...]).astype(BF16)


def _mla_attn(q, k, v, t):
    b, nh, s, dq = q.shape
    dv = v.shape[-1]
    return pl.pallas_call(
        functools.partial(_mla_attn_kernel, t=t),
        grid=(b, nh, s // t),
        in_specs=[pl.BlockSpec((1, 1, t, dq), lambda i, h, j: (i, h, j, 0)),
                  pl.BlockSpec((1, 1, s, dq), lambda i, h, j: (i, h, 0, 0)),
                  pl.BlockSpec((1, 1, s, dv), lambda i, h, j: (i, h, 0, 0))],
        out_specs=pl.BlockSpec((1, t, dv), lambda i, h, j: (i, j, h)),
        out_shape=jax.ShapeDtypeStruct((b, s, nh * dv), BF16),
        scratch_shapes=[pltpu.VMEM((t, LANES), F32), pltpu.VMEM((t, LANES), F32), pltpu.VMEM((t, dv), F32),
                        pltpu.VMEM((t, t), F32), pltpu.VMEM((t, t), F32)],
        compiler_params=_params("arbitrary", "arbitrary", "arbitrary"),
        name="mla_attn",
    )(q, k, v)


def _rec_consts(dk, dv):
    hk, hv = HEADS * dk, HEADS * dv
    c, blk = REC_CHUNK, REC_BLOCK
    head_k = np.arange(hk) // dk
    head_v = np.arange(hv) // dv
    head_r = np.repeat(np.arange(HEADS), blk)
    ltri = (np.arange(c)[:, None] >= np.arange(c)[None, :])
    ind = head_k[:, None] == head_v[None, :]
    indv = (head_v[:, None] == head_v[None, :]) / dv
    hm_k = head_r[:, None] == head_k[None, :]
    hm_v = head_r[:, None] == head_v[None, :]
    return (jnp.asarray(ltri, BF16), jnp.asarray(ind, BF16), jnp.asarray(indv, BF16),
            jnp.asarray(hm_k, F32), jnp.asarray(hm_v, F32), jnp.asarray(ind.T, F32))


def _rec_tile(q, k, v, g, st_ref, p_sc, o_sc, ltri_ref, ind_ref, hmk_ref, hmv_ref, hms_ref):
    ts, hk = q.shape
    hv = v.shape[1]
    c, blk = REC_CHUNK, REC_BLOCK
    nb = c // blk
    half = blk // 2
    chunks = [slice(i * c, (i + 1) * c) for i in range(ts // c)]
    g2 = g * LOG2_E
    g_hi = g2.astype(BF16)
    g_lo = (g2 - g_hi.astype(F32)).astype(BF16)
    ltri = ltri_ref[...]
    b = [_dot(ltri, g_hi[r]) + _dot(ltri, g_lo[r]) for r in chunks]

    sub = lax.broadcasted_iota(jnp.int32, (nb, half, hk), 1)
    for ci, r in enumerate(chunks):
        b4 = b[ci].reshape(nb, 2, half, hk)
        q4 = q[r].reshape(nb, 2, half, hk)
        k4 = k[r].reshape(nb, 2, half, hk)
        for jj in range(blk):
            hf, jr = divmod(jj, half)
            bj = b4[:, hf, jr:jr + 1, :]
            kj = k4[:, hf, jr:jr + 1, :]
            p = jnp.where(sub >= jr, q4[:, hf] * kj * jnp.exp2(b4[:, hf] - bj), 0.0)
            if hf == 0:
                p_sc[ci, jj * c:jj * c + c // 2, :] = p.reshape(c // 2, hk).astype(BF16)
                p = q4[:, 1] * kj * jnp.exp2(b4[:, 1] - bj)
                p_sc[ci, jj * c + c // 2:(jj + 1) * c, :] = p.reshape(c // 2, hk).astype(BF16)
            else:
                base = half * c + (jj - half) * (c // 2)
                p_sc[ci, base:base + c // 2, :] = p.reshape(c // 2, hk).astype(BF16)
    for ci, r in enumerate(chunks):
        s = _dot(p_sc[ci], ind_ref[...])
        v4 = v[r].reshape(nb, 2, half, hv)
        o_lo = jnp.zeros((nb, half, hv), F32)
        o_hi = jnp.zeros((nb, half, hv), F32)
        for jj in range(blk):
            hf, jr = divmod(jj, half)
            vj = v4[:, hf, jr:jr + 1, :]
            if hf == 0:
                o_lo = o_lo + s[jj * c:jj * c + c // 2, :].reshape(nb, half, hv) * vj
                o_hi = o_hi + s[jj * c + c // 2:(jj + 1) * c, :].reshape(nb, half, hv) * vj
            else:
                base = half * c + (jj - half) * (c // 2)
                o_hi = o_hi + s[base:base + c // 2, :].reshape(nb, half, hv) * vj
        o_sc[r, :] = jnp.stack([o_lo, o_hi], axis=1).reshape(c, hv)

    hmk = hmk_ref[...]
    hmv = hmv_ref[...]
    for jb in range(nb - 1):
        lo, hi = blk * jb, blk * (jb + 1)
        for ci, r in enumerate(chunks):
            qc, kc, vc, bc = q[r], k[r], v[r], b[ci]
            b_end = bc[hi - 1:hi, :]
            qj = (qc[hi:, :] * jnp.exp2(bc[hi:, :] - b_end)).astype(BF16)
            kj = kc[lo:hi, :] * jnp.exp2(b_end - bc[lo:hi, :])
            kbd = (jnp.concatenate([kj] * HEADS, axis=0) * hmk).astype(BF16)
            vbd = (jnp.concatenate([vc[lo:hi, :]] * HEADS, axis=0) * hmv).astype(BF16)
            a = _dot_nt(qj, kbd)
            o_sc[r.start + hi:r.stop, :] += _dot(a.astype(BF16), vbd)

    upd = []
    for ci, r in enumerate(chunks):
        kh = (k[r] * jnp.exp2(b[ci][c - 1:c, :] - b[ci])).astype(BF16)
        upd.append(lax.dot_general(v[r].astype(BF16), kh, TN_DIMS, preferred_element_type=F32))
    st = st_ref[...]
    for ci, r in enumerate(chunks):
        o_sc[r, :] += _dot_nt((q[r] * jnp.exp2(b[ci])).astype(BF16), st.astype(BF16))
        st = st * jnp.exp2(b[ci][c - 1:c, :]) + upd[ci] * hms_ref[...]
    st_ref[...] = st


def _head_rms_gate(o, gate, gain, indv_ref):
    ms = _dot((o * o).astype(BF16), indv_ref[...])
    return o * lax.rsqrt(ms + EPS) * gain * (gate * _sigmoid(gate))


def _gla_kernel(pb_ref, wg_ref, bg_ref, gn_ref, ltri_ref, ind_ref, indv_ref, hmk_ref, hmv_ref, hms_ref,
                y_ref, st_ref, p_sc, o_sc):
    @pl.when(pl.program_id(1) == 0)
    def _():
        st_ref[...] = jnp.zeros(st_ref.shape, F32)

    hk, hv = HEADS * GLA_DK, HEADS * GLA_DV
    x = pb_ref[0]
    q = x[:, :hk] * (GLA_DK ** -0.5)
    k = x[:, hk:2 * hk]
    v = x[:, 2 * hk:2 * hk + hv]
    gate = x[:, 2 * hk + hv:2 * hk + 2 * hv]
    logits = _dot(x[:, 2 * hk + 2 * hv:].astype(BF16), wg_ref[...]) + bg_ref[...]
    g = _log_sigmoid(logits) * (1.0 / GLA_TAU)
    _rec_tile(q, k, v, g, st_ref, p_sc, o_sc, ltri_ref, ind_ref, hmk_ref, hmv_ref, hms_ref)
    y_ref[0] = _head_rms_gate(o_sc[...], gate, gn_ref[...], indv_ref).astype(BF16)


def _hgrn_kernel(pc_ref, lb_ref, gn_ref, ltri_ref, ind_ref, indv_ref, hmk_ref, hmv_ref, hms_ref,
                 y_ref, st_ref, p_sc, o_sc):
    @pl.when(pl.program_id(1) == 0)
    def _():
        st_ref[...] = jnp.zeros(st_ref.shape, F32)

    hk, hv = HEADS * HGRN_DK, HEADS * HGRN_DV
    x = pc_ref[0]
    cq = x[:, :hk]
    z = x[:, hk:2 * hk]
    v = x[:, 2 * hk:2 * hk + hv]
    gate = x[:, 2 * hk + hv:]
    lb = lb_ref[...]
    f = jnp.maximum(lb, LB_FLOOR) + (1.0 - lb) * _sigmoid(z)
    g = jnp.minimum(jnp.log(f), 0.0)
    k = (1.0 - lb) * _sigmoid(-z)
    q = cq * _sigmoid(cq) * (HGRN_DK ** -0.5)
    _rec_tile(q, k, v, g, st_ref, p_sc, o_sc, ltri_ref, ind_ref, hmk_ref, hmv_ref, hms_ref)
    y_ref[0] = _head_rms_gate(o_sc[...], gate, gn_ref[...], indv_ref).astype(BF16)


def _recurrence(kernel, name, x, small, consts, dk, dv, ts):
    b, s, width = x.shape
    hk, hv = HEADS * dk, HEADS * dv
    ltri, ind, indv, hmk, hmv, hms = consts
    operands = (x, *small, ltri, ind, indv, hmk, hmv, hms)
    in_specs = [pl.BlockSpec((1, ts, width), lambda i, t: (i, t, 0))]
    in_specs += [_const_spec(a.shape) for a in operands[1:]]
    return pl.pallas_call(
        kernel,
        grid=(b, s // ts),
        in_specs=in_specs,
        out_specs=pl.BlockSpec((1, ts, hv), lambda i, t: (i, t, 0)),
        out_shape=jax.ShapeDtypeStruct((b, s, hv), BF16),
        scratch_shapes=[pltpu.VMEM((hv, hk), F32),
                        pltpu.VMEM((ts // REC_CHUNK, 3 * REC_BLOCK * REC_CHUNK // 4, hk), BF16),
                        pltpu.VMEM((ts, hv), F32)],
        compiler_params=_params("arbitrary", "arbitrary"),
        name=name,
    )(*operands)


def _mixca_kernel(h_ref, ya_ref, yb_ref, yc_ref, wa_ref, wb_ref, wc_ref, g_ref, wq_ref, gq_ref,
                  km_ref, vm_ref, wo_ref, out_ref):
    h1 = (h_ref[0] + _dot(ya_ref[0], wa_ref[...]) + _dot(yb_ref[0], wb_ref[...])
          + _dot(yc_ref[0], wc_ref[...]))
    qc = _dot(_rms(h1, g_ref[...]).astype(BF16), wq_ref[...])
    heads = []
    for h in range(HEADS):
        qn = _rms(qc[:, CA_DIM * h:CA_DIM * (h + 1)], gq_ref[...]) * (CA_DIM ** -0.5)
        s = _dot_nt(qn.astype(BF16), km_ref[0, h])
        p = jnp.exp(s - jnp.max(s, axis=-1, keepdims=True))
        o = _dot(p.astype(BF16), vm_ref[0, h]) / jnp.sum(p, axis=-1, keepdims=True)
        heads.append(o.astype(BF16))
    out_ref[0] = h1 + _dot(jnp.concatenate(heads, axis=-1), wo_ref[...])


def _mixca(h, ya, yb, yc, wa, wb, wc, g, wq, gq, km, vm, wo, tm):
    b, s, d = h.shape
    tok = lambda last: pl.BlockSpec((1, tm, last), lambda i, t: (i, t, 0))
    mem = pl.BlockSpec((1,) + km.shape[1:], lambda i, t: (i, 0, 0, 0))
    return pl.pallas_call(
        _mixca_kernel,
        grid=(b, s // tm),
        in_specs=[tok(d), tok(ya.shape[-1]), tok(yb.shape[-1]), tok(yc.shape[-1]),
                  _const_spec(wa.shape), _const_spec(wb.shape), _const_spec(wc.shape),
                  _const_spec(g.shape), _const_spec(wq.shape), _const_spec(gq.shape),
                  mem, mem, _const_spec(wo.shape)],
        out_specs=tok(d),
        out_shape=jax.ShapeDtypeStruct((b, s, d), F32),
        compiler_params=_params("arbitrary", "arbitrary"),
        name="mix_ca",
    )(h, ya, yb, yc, wa, wb, wc, g, wq, gq, km, vm, wo)


def _ffn_kernel(h_ref, g_ref, wg_ref, wu_ref, cw_ref, cb_ref, wo_ref, out_ref, gs_sc, carry_sc, *, fc):
    tm = h_ref.shape[1]
    d_ff = wg_ref.shape[1]

    @pl.when(pl.program_id(1) == 0)
    def _():
        carry_sc[...] = jnp.zeros(carry_sc.shape, F32)

    x = h_ref[0]
    n = _rms(x, g_ref[...]).astype(BF16)
    acc = x
    for c0 in range(0, d_ff, fc):
        gate = _dot(n, wg_ref[:, c0:c0 + fc])
        gs_sc[0:8, :] = carry_sc[:, c0:c0 + fc]
        gs_sc[8:8 + tm, :] = gate
        carry_sc[:, c0:c0 + fc] = gate[tm - 8:, :]
        conv = (cb_ref[:, c0:c0 + fc] + gs_sc[6:6 + tm, :] * cw_ref[0:1, c0:c0 + fc]
                + gs_sc[7:7 + tm, :] * cw_ref[1:2, c0:c0 + fc] + gate * cw_ref[2:3, c0:c0 + fc])
        up = _dot(n, wu_ref[:, c0:c0 + fc])
        act = (conv * _sigmoid(conv) * up).astype(BF16)
        acc = acc + _dot(act, wo_ref[c0:c0 + fc, :])
    out_ref[0] = acc


def _ffn(h, g, wg, wu, cw, cb, wo, tm, fc):
    b, s, d = h.shape
    d_ff = wg.shape[1]
    tok = pl.BlockSpec((1, tm, d), lambda i, t: (i, t, 0))
    return pl.pallas_call(
        functools.partial(_ffn_kernel, fc=fc),
        grid=(b, s // tm),
        in_specs=[tok, _const_spec(g.shape), _const_spec(wg.shape), _const_spec(wu.shape),
                  _const_spec(cw.shape), _const_spec(cb.shape), _const_spec(wo.shape)],
        out_specs=tok,
        out_shape=jax.ShapeDtypeStruct((b, s, d), F32),
        scratch_shapes=[pltpu.VMEM((tm + 8, fc), F32), pltpu.VMEM((8, d_ff), F32)],
        compiler_params=_params("arbitrary", "arbitrary"),
        name="ffn",
    )(h, g, wg, wu, cw, cb, wo)


def _rope_lanes(x):
    half = MLA_ROPE // 2
    z = jnp.zeros(x.shape[:-1] + (LANES // 2 - half,), x.dtype)
    return jnp.concatenate([x[..., :half], z, x[..., half:], z], axis=-1)


def _rope_lanes_swapped(x):
    half = MLA_ROPE // 2
    return _rope_lanes(jnp.concatenate([x[..., half:], x[..., :half]], axis=-1))


def _qk_lanes(x):
    rope = x[..., MLA_NOPE:]
    return jnp.concatenate([x[..., :MLA_NOPE], _rope_lanes(rope), _rope_lanes_swapped(rope)], axis=-1)


def _relayout_w_in(w):
    offs = np.cumsum([0, MLA_Q_LORA, MLA_KV_LORA, MLA_ROPE, HEADS * GLA_DK, HEADS * GLA_DK, HEADS * GLA_DV,
                      GLA_RANK, HEADS * GLA_DV, HEADS * HGRN_DK, HEADS * HGRN_DK, HEADS * HGRN_DV,
                      HEADS * HGRN_DV])
    seg = lambda i: w[:, offs[i]:offs[i + 1]]
    gate_lr = jnp.pad(seg(6), ((0, 0), (0, LANES - GLA_RANK)))
    return jnp.concatenate([seg(0), seg(1), _rope_lanes(seg(2)), _rope_lanes_swapped(seg(2)),
                            seg(3), seg(4), seg(5), seg(7), gate_lr,
                            seg(8), seg(9), seg(10), seg(11)], axis=-1)


def _pick(n, pref):
    return pref if n % pref == 0 else n


def kernel(x, mem, positions, norm_mix, w_in, mla_q_lat_norm, w_q_up, mla_kv_lat_norm, w_kv_up, mla_q_norm,
           mla_k_norm, gla_w_gate, gla_b_gate, gla_out_norm, hgrn_lb_logits, hgrn_out_norm, w_mix_out,
           norm_ca, norm_mem, w_ca_q, w_ca_kv, ca_q_norm, ca_k_norm, w_ca_out, norm_ffn, w_ffn_in,
           ffn_conv_w, ffn_conv_b, w_ffn_out):
    b, s, d = x.shape
    depth = w_in.shape[0]
    d_ff = w_ffn_out.shape[1]
    tm = _pick(s, 512)
    t_attn = _pick(s, 512)
    ts_rec = _pick(s, 512)
    fc = d_ff // 2

    gamma = jax.nn.softmax(hgrn_lb_logits.astype(F32), axis=0)
    lower_bounds = jnp.cumsum(gamma, axis=0) - gamma[0]
    inv_freq = ROPE_THETA ** (-jnp.arange(0, MLA_ROPE, 2, dtype=F32) / MLA_ROPE)
    ang = positions.astype(F32)[..., None] * inv_freq
    cos = _rope_lanes(jnp.concatenate([jnp.cos(ang), jnp.cos(ang)], axis=-1))
    sin = _rope_lanes(jnp.concatenate([-jnp.sin(ang), jnp.sin(ang)], axis=-1))

    km, vm = _memkv(mem, norm_mem, w_ca_kv.astype(BF16), ca_k_norm)
    gla_consts = _rec_consts(GLA_DK, GLA_DV)
    hgrn_consts = _rec_consts(HGRN_DK, HGRN_DV)
    row = lambda a: a.reshape(1, -1)
    n_a = HEADS * MLA_V
    n_b = HEADS * GLA_DV

    h = x
    for l in range(depth):
        w_l = _relayout_w_in(w_in[l]).astype(BF16)
        wq_l = _qk_lanes(w_q_up[l].reshape(MLA_Q_LORA, HEADS, MLA_QK)).reshape(MLA_Q_LORA, -1).astype(BF16)
        q, k, v, pb, pc = _inproj(
            h, row(norm_mix[l]), w_l, row(mla_q_lat_norm[l]), wq_l, row(mla_kv_lat_norm[l]),
            w_kv_up[l].astype(BF16), row(_qk_lanes(mla_q_norm[l])), row(_qk_lanes(mla_k_norm[l])),
            cos, sin, tm)
        ya = _mla_attn(q, k, v, t_attn)
        wg_l = jnp.pad(gla_w_gate[l], ((0, LANES - GLA_RANK), (0, 0))).astype(BF16)
        yb = _recurrence(_gla_kernel, "gla", pb,
                         (wg_l, row(gla_b_gate[l]), row(jnp.tile(gla_out_norm[l], HEADS))),
                         gla_consts, GLA_DK, GLA_DV, ts_rec)
        yc = _recurrence(_hgrn_kernel, "hgrn", pc,
                         (row(lower_bounds[l]), row(jnp.tile(hgrn_out_norm[l], HEADS))),
                         hgrn_consts, HGRN_DK, HGRN_DV, ts_rec)
        wm = w_mix_out[l].astype(BF16)
        h = _mixca(h, ya, yb, yc, wm[:n_a], wm[n_a:n_a + n_b], wm[n_a + n_b:], row(norm_ca[l]),
                   w_ca_q[l].astype(BF16), row(ca_q_norm[l]), km[l], vm[l], w_ca_out[l].astype(BF16), tm)
        wf = w_ffn_in[l].astype(BF16)
        cw = jnp.pad(ffn_conv_w[l], ((0, 8 - ffn_conv_w.shape[1]), (0, 0)))
        h = _ffn(h, row(norm_ffn[l]), wf[:, :d_ff], wf[:, d_ff:], cw, row(ffn_conv_b[l]),
                 w_ffn_out[l].astype(BF16), tm, fc)
    return h
```

```python
import functools

import numpy as np
import jax
import jax.numpy as jnp
from jax import lax
from jax.experimental import pallas as pl
from jax.experimental.pallas import tpu as pltpu

F32 = jnp.float32
BF16 = jnp.bfloat16

EPS = 1e-6
MASK_VALUE = -1e30
LB_FLOOR = 1e-30
CHUNK = 64
ROPE_THETA = 10000.0
LOG2_E = 1.4426950408889634

HEADS = 4
MLA_Q_LORA = 256
MLA_KV_LORA = 256
MLA_NOPE = 128
MLA_ROPE = 64
MLA_V = 128
MLA_QK = MLA_NOPE + MLA_ROPE
MLA_QK_PAD = 256
GLA_DK = 32
GLA_DV = 64
GLA_RANK = 16
GLA_TAU = 16.0
HGRN_DK = 64
HGRN_DV = 64
CA_DIM = 128
MLA_Q_COLS = 384
N_A = 768
N_B = 896
N_C = 1024

LANES = 128
ATTN_UNROLL = 8
REC_CHUNK = 128
REC_BLOCK = 16
VMEM_LIMIT = 56 * 1024 * 1024

NT_DIMS = (((1,), (1,)), ((), ()))
TN_DIMS = (((0,), (0,)), ((), ()))


def _dot(a, b):
    return jnp.dot(a, b, preferred_element_type=F32)


def _dot_nt(a, b):
    return lax.dot_general(a, b, NT_DIMS, preferred_element_type=F32)


def _rms(x, gain, dim=None):
    dim = x.shape[-1] if dim is None else dim
    ms = jnp.sum(x * x, axis=-1, keepdims=True) * (1.0 / dim)
    return x * lax.rsqrt(ms + EPS) * gain


def _sigmoid(x):
    return 1.0 / (1.0 + jnp.exp(-x))


def _log_sigmoid(x):
    return jnp.minimum(x, 0.0) - jnp.log1p(jnp.exp(-jnp.abs(x)))


def _params(*sem):
    return pltpu.CompilerParams(dimension_semantics=sem, vmem_limit_bytes=VMEM_LIMIT)


def _const_spec(shape):
    nd = len(shape)
    return pl.BlockSpec(shape, lambda *_: (0,) * nd)


def _memkv_kernel(mem_ref, g_ref, w_ref, gk_ref, k_ref, v_ref):
    n = _rms(mem_ref[0], g_ref[0]).astype(BF16)
    kv = _dot(n, w_ref[0])
    for h in range(HEADS):
        base = 2 * CA_DIM * h
        k_ref[0, 0, h] = _rms(kv[:, base:base + CA_DIM], gk_ref[0]).astype(BF16)
        v_ref[0, 0, h] = kv[:, base + CA_DIM:base + 2 * CA_DIM].astype(BF16)


def _memkv(mem, norm_mem, w_ca_kv, ca_k_norm):
    depth, d = norm_mem.shape
    b, m, _ = mem.shape
    out = jax.ShapeDtypeStruct((depth, b, HEADS, m, CA_DIM), BF16)
    spec_o = pl.BlockSpec((1, 1, HEADS, m, CA_DIM), lambda l, i: (l, i, 0, 0, 0))
    return pl.pallas_call(
        _memkv_kernel,
        grid=(depth, b),
        in_specs=[
            pl.BlockSpec((1, m, d), lambda l, i: (i, 0, 0)),
            pl.BlockSpec((1, 1, d), lambda l, i: (l, 0, 0)),
            pl.BlockSpec((1, d, w_ca_kv.shape[-1]), lambda l, i: (l, 0, 0)),
            pl.BlockSpec((1, 1, CA_DIM), lambda l, i: (l, 0, 0)),
        ],
        out_specs=[spec_o, spec_o],
        out_shape=[out, out],
        compiler_params=_params("arbitrary", "arbitrary"),
        name="memkv",
    )(mem, norm_mem.reshape(depth, 1, d), w_ca_kv, ca_k_norm.reshape(depth, 1, CA_DIM))


def _inproj_kernel(h_ref, g_ref, w_ref, gql_ref, wq_ref, gkvl_ref, wkv_ref, gq_ref, gk_ref,
                   cos_ref, sin_ref, ones_ref, bd_ref, q_ref, k_ref, v_ref, pb_ref, pc_ref):
    n = _rms(h_ref[0], g_ref[...]).astype(BF16)
    pa = _dot(n, w_ref[:, :N_A])
    pb_ref[0] = _dot(n, w_ref[:, N_A:N_A + N_B])
    pc_ref[0] = _dot(n, w_ref[:, N_A + N_B:])

    def latent(x, gain):
        ms = _dot((x * x).astype(BF16), ones_ref[...]) * (1.0 / x.shape[1])
        return (x * lax.rsqrt(ms + EPS) * gain).astype(BF16)

    qf = _dot(latent(pa[:, :MLA_Q_LORA], gql_ref[...]), wq_ref[...])
    kvf = _dot(latent(pa[:, MLA_Q_LORA:MLA_Q_LORA + MLA_KV_LORA], gkvl_ref[...]), wkv_ref[...])
    kr = pa[:, N_A - 2 * LANES:N_A - LANES]
    kr_sw = pa[:, N_A - LANES:]
    cos = cos_ref[0]
    sin = sin_ref[0]
    gq = gq_ref[...]
    gk = gk_ref[...]
    scale = MLA_QK ** -0.5 * LOG2_E
    gq_nope = gq[:, :LANES] * scale
    q_cos = gq[:, LANES:2 * LANES] * cos * scale
    q_sin = gq[:, 2 * LANES:] * sin * scale
    k_rot = kr * (gk[:, LANES:2 * LANES] * cos) + kr_sw * (gk[:, 2 * LANES:] * sin)
    kr_sq = kr * kr
    q_sq = jnp.concatenate(
        [qf[:, MLA_Q_COLS * h:MLA_Q_COLS * h + LANES] ** 2
         + qf[:, MLA_Q_COLS * h + LANES:MLA_Q_COLS * h + 2 * LANES] ** 2 for h in range(HEADS)], axis=1)
    k_sq = jnp.concatenate(
        [kvf[:, 2 * LANES * h:2 * LANES * h + LANES] ** 2 + kr_sq for h in range(HEADS)], axis=1)
    q_rs = lax.rsqrt(_dot(q_sq.astype(BF16), bd_ref[...]) * (1.0 / MLA_QK) + EPS)
    k_rs = lax.rsqrt(_dot(k_sq.astype(BF16), bd_ref[...]) * (1.0 / MLA_QK) + EPS)
    for h in range(HEADS):
        qb = MLA_Q_COLS * h
        rs = q_rs[:, LANES * h:LANES * (h + 1)]
        q_ref[0, h, :, :MLA_NOPE] = (qf[:, qb:qb + LANES] * rs * gq_nope).astype(BF16)
        q_ref[0, h, :, MLA_NOPE:] = ((qf[:, qb + LANES:qb + 2 * LANES] * q_cos
                                      + qf[:, qb + 2 * LANES:qb + 3 * LANES] * q_sin) * rs).astype(BF16)
        kb = 2 * LANES * h
        rs = k_rs[:, LANES * h:LANES * (h + 1)]
        k_ref[0, h, :, :MLA_NOPE] = (kvf[:, kb:kb + LANES] * rs * gk[:, :LANES]).astype(BF16)
        k_ref[0, h, :, MLA_NOPE:] = (k_rot * rs).astype(BF16)
        v_ref[0, h] = kvf[:, kb + LANES:kb + 2 * LANES].astype(BF16)


def _inproj(h, g, w, gql, wq, gkvl, wkv, gq, gk, cos, sin, tm):
    ones = jnp.ones((MLA_Q_LORA, MLA_Q_LORA), BF16)
    head_of = np.arange(HEADS * LANES) // LANES
    bd = jnp.asarray(head_of[:, None] == head_of[None, :], BF16)
    b, s, d = h.shape
    grid = (b, s // tm)
    tok = lambda last: pl.BlockSpec((1, tm, last), lambda i, t: (i, t, 0))
    head = lambda last: pl.BlockSpec((1, HEADS, tm, last), lambda i, t: (i, 0, t, 0))
    return pl.pallas_call(
        _inproj_kernel,
        grid=grid,
        in_specs=[tok(d), _const_spec(g.shape), _const_spec(w.shape), _const_spec(gql.shape),
                  _const_spec(wq.shape), _const_spec(gkvl.shape), _const_spec(wkv.shape),
                  _const_spec(gq.shape), _const_spec(gk.shape), tok(LANES), tok(LANES),
                  _const_spec(ones.shape), _const_spec(bd.shape)],
        out_specs=[head(MLA_QK_PAD), head(MLA_QK_PAD), head(MLA_V), tok(N_B), tok(N_C)],
        out_shape=[jax.ShapeDtypeStruct((b, HEADS, s, MLA_QK_PAD), BF16),
                   jax.ShapeDtypeStruct((b, HEADS, s, MLA_QK_PAD), BF16),
                   jax.ShapeDtypeStruct((b, HEADS, s, MLA_V), BF16),
                   jax.ShapeDtypeStruct((b, s, N_B), F32),
                   jax.ShapeDtypeStruct((b, s, N_C), F32)],
        compiler_params=_params("arbitrary", "arbitrary"),
        name="inproj",
    )(h, g, w, gql, wq, gkvl, wkv, gq, gk, cos, sin, ones, bd)


def _mla_attn_kernel(q_ref, k_ref, v_ref, o_ref, m_sc, l_sc, acc_sc, sa_sc, sb_sc, *, t):
    qi = pl.program_id(2)
    q = q_ref[0, 0]
    m_sc[...] = jnp.full(m_sc.shape, MASK_VALUE, F32)
    l_sc[...] = jnp.zeros(l_sc.shape, F32)
    acc_sc[...] = jnp.zeros(acc_sc.shape, F32)

    def scores(j):
        return _dot_nt(q, k_ref[0, 0, pl.ds(pl.multiple_of(j * t, t), t), :])

    def update(s, j):
        m_prev = m_sc[...]
        m_new = jnp.maximum(m_prev, jnp.max(s, axis=-1, keepdims=True))
        alpha = jnp.exp2(m_prev - m_new)
        p = jnp.exp2(s - jnp.concatenate([m_new] * (t // LANES), axis=1))
        l_sc[...] = alpha * l_sc[...] + jnp.sum(p, axis=-1, keepdims=True)
        v_tile = v_ref[0, 0, pl.ds(pl.multiple_of(j * t, t), t), :]
        acc_sc[...] = alpha * acc_sc[...] + _dot(p.astype(BF16), v_tile)
        m_sc[...] = m_new

    def update_diag(s):
        qc = lax.broadcasted_iota(jnp.int32, (t, t), 0) // CHUNK
        kc = lax.broadcasted_iota(jnp.int32, (t, t), 1) // CHUNK
        update(jnp.where(kc <= qc, s, MASK_VALUE), qi)

    bufs = (sa_sc, sb_sc)
    sa_sc[...] = scores(0)

    def group(g, carry):
        j0 = ATTN_UNROLL * g
        for u in range(ATTN_UNROLL):
            bufs[(u + 1) % 2][...] = scores(j0 + u + 1)
            update(bufs[u % 2][...], j0 + u)
        return carry

    lax.fori_loop(0, qi // ATTN_UNROLL, group, 0)

    def tail(rest):
        def run():
            for i in range(rest + 1):
                if i < rest:
                    bufs[(i + 1) % 2][...] = scores(qi - rest + i + 1)
                    update(bufs[i % 2][...], qi - rest + i)
                else:
                    update_diag(bufs[i % 2][...])
        return run

    for rest in range(ATTN_UNROLL):
        pl.when(qi % ATTN_UNROLL == rest)(tail(rest))

    o_ref[0] = (acc_sc[...] / l_sc[...]).astype(BF16)


def _mla_attn(q, k, v, t):
    b, nh, s, dq = q.shape
    dv = v.shape[-1]
    return pl.pallas_call(
        functools.partial(_mla_attn_kernel, t=t),
        grid=(b, nh, s // t),
        in_specs=[pl.BlockSpec((1, 1, t, dq), lambda i, h, j: (i, h, j, 0)),
                  pl.BlockSpec((1, 1, s, dq), lambda i, h, j: (i, h, 0, 0)),
                  pl.BlockSpec((1, 1, s, dv), lambda i, h, j: (i, h, 0, 0))],
        out_specs=pl.BlockSpec((1, t, dv), lambda i, h, j: (i, j, h)),
        out_shape=jax.ShapeDtypeStruct((b, s, nh * dv), BF16),
        scratch_shapes=[pltpu.VMEM((t, LANES), F32), pltpu.VMEM((t, LANES), F32), pltpu.VMEM((t, dv), F32),
                        pltpu.VMEM((t, t), F32), pltpu.VMEM((t, t), F32)],
        compiler_params=_params("arbitrary", "arbitrary", "arbitrary"),
        name="mla_attn",
    )(q, k, v)


def _rec_consts(dk, dv):
    hk, hv = HEADS * dk, HEADS * dv
    c, blk = REC_CHUNK, REC_BLOCK
    head_k = np.arange(hk) // dk
    head_v = np.arange(hv) // dv
    head_r = np.repeat(np.arange(HEADS), blk)
    ltri = (np.arange(c)[:, None] >= np.arange(c)[None, :])
    ind = head_k[:, None] == head_v[None, :]
    col = (head_k[None, :] * blk + np.arange(blk)[:, None]).reshape(-1)
    ind_diag = col[:, None] == np.arange(HEADS * blk)[None, :]
    indv = (head_v[:, None] == head_v[None, :]) / dv
    hm_k = head_r[:, None] == head_k[None, :]
    hm_v = head_r[:, None] == head_v[None, :]
    return (jnp.asarray(ltri, BF16), jnp.asarray(ind_diag, BF16), jnp.asarray(indv, BF16),
            jnp.asarray(hm_k, F32), jnp.asarray(hm_v, F32), jnp.asarray(ind.T, F32))


def _rec_tile(q, k, v, g, st_ref, plo_sc, phi_sc, o_sc, ltri_ref, ind_ref, hmk_ref, hmv_ref, hms_ref):
    ts, hk = q.shape
    hv = v.shape[1]
    c, blk = REC_CHUNK, REC_BLOCK
    nb = c // blk
    half = blk // 2
    chunks = [slice(i * c, (i + 1) * c) for i in range(ts // c)]
    g2 = g * LOG2_E
    g_hi = g2.astype(BF16)
    g_lo = (g2 - g_hi.astype(F32)).astype(BF16)
    ltri = ltri_ref[...]
    b = [_dot(ltri, g_hi[r]) + _dot(ltri, g_lo[r]) for r in chunks]

    sub = lax.broadcasted_iota(jnp.int32, (nb, half, hk), 1)
    diag = []
    for ci, r in enumerate(chunks):
        b4 = b[ci].reshape(nb, 2, half, hk)
        q4 = q[r].reshape(nb, 2, half, hk)
        k4 = k[r].reshape(nb, 2, half, hk)
        for jj in range(blk):
            hf, jr = divmod(jj, half)
            bj = b4[:, hf, jr:jr + 1, :]
            kj = k4[:, hf, jr:jr + 1, :]
            p = jnp.where(sub >= jr, q4[:, hf] * kj * jnp.exp2(b4[:, hf] - bj), 0.0)
            if hf == 0:
                plo_sc[ci, :, jj * hk:(jj + 1) * hk] = p.reshape(c // 2, hk).astype(BF16)
                p = q4[:, 1] * kj * jnp.exp2(b4[:, 1] - bj)
            phi_sc[ci, :, jj * hk:(jj + 1) * hk] = p.reshape(c // 2, hk).astype(BF16)
        d_lo = _dot(plo_sc[ci], ind_ref[:half * hk, :]).reshape(nb, half, HEADS * blk)
        d_hi = _dot(phi_sc[ci], ind_ref[...]).reshape(nb, half, HEADS * blk)
        diag.append(jnp.stack([d_lo, d_hi], axis=1).reshape(c, HEADS * blk))

    hmk = hmk_ref[...]
    hmv = hmv_ref[...]
    for jb in range(nb):
        lo, hi = blk * jb, blk * (jb + 1)
        for ci, r in enumerate(chunks):
            qc, kc, vc, bc = q[r], k[r], v[r], b[ci]
            a = diag[ci][lo:hi, :]
            if hi < c:
                b_end = bc[hi - 1:hi, :]
                qj = (qc[hi:, :] * jnp.exp2(bc[hi:, :] - b_end)).astype(BF16)
                kj = kc[lo:hi, :] * jnp.exp2(b_end - bc[lo:hi, :])
                kbd = (jnp.concatenate([kj] * HEADS, axis=0) * hmk).astype(BF16)
                a = jnp.concatenate([a, _dot_nt(qj, kbd)], axis=0)
            vbd = (jnp.concatenate([vc[lo:hi, :]] * HEADS, axis=0) * hmv).astype(BF16)
            contrib = _dot(a.astype(BF16), vbd)
            if jb == 0:
                o_sc[r, :] = contrib
            else:
                o_sc[r.start + lo:r.stop, :] += contrib

    upd = []
    for ci, r in enumerate(chunks):
        kh = (k[r] * jnp.exp2(b[ci][c - 1:c, :] - b[ci])).astype(BF16)
        upd.append(lax.dot_general(v[r].astype(BF16), kh, TN_DIMS, preferred_element_type=F32))
    st = st_ref[...]
    for ci, r in enumerate(chunks):
        o_sc[r, :] += _dot_nt((q[r] * jnp.exp2(b[ci])).astype(BF16), st.astype(BF16))
        st = st * jnp.exp2(b[ci][c - 1:c, :]) + upd[ci] * hms_ref[...]
    st_ref[...] = st


def _head_rms_gate(o, gate, gain, indv_ref):
    ms = _dot((o * o).astype(BF16), indv_ref[...])
    return o * lax.rsqrt(ms + EPS) * gain * (gate * _sigmoid(gate))


def _gla_kernel(pb_ref, wg_ref, bg_ref, gn_ref, ltri_ref, ind_ref, indv_ref, hmk_ref, hmv_ref, hms_ref,
                y_ref, st_ref, plo_sc, phi_sc, o_sc):
    @pl.when(pl.program_id(1) == 0)
    def _():
        st_ref[...] = jnp.zeros(st_ref.shape, F32)

    hk, hv = HEADS * GLA_DK, HEADS * GLA_DV
    x = pb_ref[0]
    q = x[:, :hk] * (GLA_DK ** -0.5)
    k = x[:, hk:2 * hk]
    v = x[:, 2 * hk:2 * hk + hv]
    gate = x[:, 2 * hk + hv:2 * hk + 2 * hv]
    logits = _dot(x[:, 2 * hk + 2 * hv:].astype(BF16), wg_ref[...]) + bg_ref[...]
    g = _log_sigmoid(logits) * (1.0 / GLA_TAU)
    _rec_tile(q, k, v, g, st_ref, plo_sc, phi_sc, o_sc, ltri_ref, ind_ref, hmk_ref, hmv_ref, hms_ref)
    y_ref[0] = _head_rms_gate(o_sc[...], gate, gn_ref[...], indv_ref).astype(BF16)


def _hgrn_kernel(pc_ref, lb_ref, gn_ref, ltri_ref, ind_ref, indv_ref, hmk_ref, hmv_ref, hms_ref,
                 y_ref, st_ref, plo_sc, phi_sc, o_sc):
    @pl.when(pl.program_id(1) == 0)
    def _():
        st_ref[...] = jnp.zeros(st_ref.shape, F32)

    hk, hv = HEADS * HGRN_DK, HEADS * HGRN_DV
    x = pc_ref[0]
    cq = x[:, :hk]
    z = x[:, hk:2 * hk]
    v = x[:, 2 * hk:2 * hk + hv]
    gate = x[:, 2 * hk + hv:]
    lb = lb_ref[...]
    f = jnp.maximum(lb, LB_FLOOR) + (1.0 - lb) * _sigmoid(z)
    g = jnp.minimum(jnp.log(f), 0.0)
    k = (1.0 - lb) * _sigmoid(-z)
    q = cq * _sigmoid(cq) * (HGRN_DK ** -0.5)
    _rec_tile(q, k, v, g, st_ref, plo_sc, phi_sc, o_sc, ltri_ref, ind_ref, hmk_ref, hmv_ref, hms_ref)
    y_ref[0] = _head_rms_gate(o_sc[...], gate, gn_ref[...], indv_ref).astype(BF16)


def _recurrence(kernel, name, x, small, consts, dk, dv, ts):
    b, s, width = x.shape
    hk, hv = HEADS * dk, HEADS * dv
    ltri, ind, indv, hmk, hmv, hms = consts
    operands = (x, *small, ltri, ind, indv, hmk, hmv, hms)
    in_specs = [pl.BlockSpec((1, ts, width), lambda i, t: (i, t, 0))]
    in_specs += [_const_spec(a.shape) for a in operands[1:]]
    return pl.pallas_call(
        kernel,
        grid=(b, s // ts),
        in_specs=in_specs,
        out_specs=pl.BlockSpec((1, ts, hv), lambda i, t: (i, t, 0)),
        out_shape=jax.ShapeDtypeStruct((b, s, hv), BF16),
        scratch_shapes=[pltpu.VMEM((hv, hk), F32),
                        pltpu.VMEM((ts // REC_CHUNK, REC_CHUNK // 2, REC_BLOCK // 2 * hk), BF16),
                        pltpu.VMEM((ts // REC_CHUNK, REC_CHUNK // 2, REC_BLOCK * hk), BF16),
                        pltpu.VMEM((ts, hv), F32)],
        compiler_params=_params("arbitrary", "arbitrary"),
        name=name,
    )(*operands)


def _mixca_kernel(h_ref, ya_ref, yb_ref, yc_ref, wa_ref, wb_ref, wc_ref, g_ref, wq_ref, gq_ref,
                  km_ref, vm_ref, wo_ref, out_ref):
    h1 = (h_ref[0] + _dot(ya_ref[0], wa_ref[...]) + _dot(yb_ref[0], wb_ref[...])
          + _dot(yc_ref[0], wc_ref[...]))
    qc = _dot(_rms(h1, g_ref[...]).astype(BF16), wq_ref[...])
    heads = []
    for h in range(HEADS):
        qn = _rms(qc[:, CA_DIM * h:CA_DIM * (h + 1)], gq_ref[...]) * (CA_DIM ** -0.5)
        s = _dot_nt(qn.astype(BF16), km_ref[0, h])
        p = jnp.exp(s - jnp.max(s, axis=-1, keepdims=True))
        o = _dot(p.astype(BF16), vm_ref[0, h]) / jnp.sum(p, axis=-1, keepdims=True)
        heads.append(o.astype(BF16))
    out_ref[0] = h1 + _dot(jnp.concatenate(heads, axis=-1), wo_ref[...])


def _mixca(h, ya, yb, yc, wa, wb, wc, g, wq, gq, km, vm, wo, tm):
    b, s, d = h.shape
    tok = lambda last: pl.BlockSpec((1, tm, last), lambda i, t: (i, t, 0))
    mem = pl.BlockSpec((1,) + km.shape[1:], lambda i, t: (i, 0, 0, 0))
    return pl.pallas_call(
        _mixca_kernel,
        grid=(b, s // tm),
        in_specs=[tok(d), tok(ya.shape[-1]), tok(yb.shape[-1]), tok(yc.shape[-1]),
                  _const_spec(wa.shape), _const_spec(wb.shape), _const_spec(wc.shape),
                  _const_spec(g.shape), _const_spec(wq.shape), _const_spec(gq.shape),
                  mem, mem, _const_spec(wo.shape)],
        out_specs=tok(d),
        out_shape=jax.ShapeDtypeStruct((b, s, d), F32),
        compiler_params=_params("arbitrary", "arbitrary"),
        name="mix_ca",
    )(h, ya, yb, yc, wa, wb, wc, g, wq, gq, km, vm, wo)


def _ffn_kernel(h_ref, g_ref, wg_ref, wu_ref, cw_ref, cb_ref, wo_ref, out_ref, gs_sc, carry_sc, *, fc):
    tm = h_ref.shape[1]
    d_ff = wg_ref.shape[1]

    @pl.when(pl.program_id(1) == 0)
    def _():
        carry_sc[...] = jnp.zeros(carry_sc.shape, F32)

    x = h_ref[0]
    n = _rms(x, g_ref[...]).astype(BF16)
    acc = x
    for c0 in range(0, d_ff, fc):
        gate = _dot(n, wg_ref[:, c0:c0 + fc])
        gs_sc[0:8, :] = carry_sc[:, c0:c0 + fc]
        gs_sc[8:8 + tm, :] = gate
        carry_sc[:, c0:c0 + fc] = gate[tm - 8:, :]
        conv = (cb_ref[:, c0:c0 + fc] + gs_sc[6:6 + tm, :] * cw_ref[0:1, c0:c0 + fc]
                + gs_sc[7:7 + tm, :] * cw_ref[1:2, c0:c0 + fc] + gate * cw_ref[2:3, c0:c0 + fc])
        up = _dot(n, wu_ref[:, c0:c0 + fc])
        act = (conv * _sigmoid(conv) * up).astype(BF16)
        acc = acc + _dot(act, wo_ref[c0:c0 + fc, :])
    out_ref[0] = acc


def _ffn(h, g, wg, wu, cw, cb, wo, tm, fc):
    b, s, d = h.shape
    d_ff = wg.shape[1]
    tok = pl.BlockSpec((1, tm, d), lambda i, t: (i, t, 0))
    return pl.pallas_call(
        functools.partial(_ffn_kernel, fc=fc),
        grid=(b, s // tm),
        in_specs=[tok, _const_spec(g.shape), _const_spec(wg.shape), _const_spec(wu.shape),
                  _const_spec(cw.shape), _const_spec(cb.shape), _const_spec(wo.shape)],
        out_specs=tok,
        out_shape=jax.ShapeDtypeStruct((b, s, d), F32),
        scratch_shapes=[pltpu.VMEM((tm + 8, fc), F32), pltpu.VMEM((8, d_ff), F32)],
        compiler_params=_params("arbitrary", "arbitrary"),
        name="ffn",
    )(h, g, wg, wu, cw, cb, wo)


def _rope_lanes(x):
    half = MLA_ROPE // 2
    z = jnp.zeros(x.shape[:-1] + (LANES // 2 - half,), x.dtype)
    return jnp.concatenate([x[..., :half], z, x[..., half:], z], axis=-1)


def _rope_lanes_swapped(x):
    half = MLA_ROPE // 2
    return _rope_lanes(jnp.concatenate([x[..., half:], x[..., :half]], axis=-1))


def _qk_lanes(x):
    rope = x[..., MLA_NOPE:]
    return jnp.concatenate([x[..., :MLA_NOPE], _rope_lanes(rope), _rope_lanes_swapped(rope)], axis=-1)


def _relayout_w_in(w):
    offs = np.cumsum([0, MLA_Q_LORA, MLA_KV_LORA, MLA_ROPE, HEADS * GLA_DK, HEADS * GLA_DK, HEADS * GLA_DV,
                      GLA_RANK, HEADS * GLA_DV, HEADS * HGRN_DK, HEADS * HGRN_DK, HEADS * HGRN_DV,
                      HEADS * HGRN_DV])
    seg = lambda i: w[:, offs[i]:offs[i + 1]]
    gate_lr = jnp.pad(seg(6), ((0, 0), (0, LANES - GLA_RANK)))
    return jnp.concatenate([seg(0), seg(1), _rope_lanes(seg(2)), _rope_lanes_swapped(seg(2)),
                            seg(3), seg(4), seg(5), seg(7), gate_lr,
                            seg(8), seg(9), seg(10), seg(11)], axis=-1)


def _pick(n, pref):
    return pref if n % pref == 0 else n


def kernel(x, mem, positions, norm_mix, w_in, mla_q_lat_norm, w_q_up, mla_kv_lat_norm, w_kv_up, mla_q_norm,
           mla_k_norm, gla_w_gate, gla_b_gate, gla_out_norm, hgrn_lb_logits, hgrn_out_norm, w_mix_out,
           norm_ca, norm_mem, w_ca_q, w_ca_kv, ca_q_norm, ca_k_norm, w_ca_out, norm_ffn, w_ffn_in,
           ffn_conv_w, ffn_conv_b, w_ffn_out):
    b, s, d = x.shape
    depth = w_in.shape[0]
    d_ff = w_ffn_out.shape[1]
    tm = _pick(s, 512)
    t_attn = _pick(s, 512)
    ts_rec = _pick(s, 512)
    fc = d_ff // 2

    gamma = jax.nn.softmax(hgrn_lb_logits.astype(F32), axis=0)
    lower_bounds = jnp.cumsum(gamma, axis=0) - gamma[0]
    inv_freq = ROPE_THETA ** (-jnp.arange(0, MLA_ROPE, 2, dtype=F32) / MLA_ROPE)
    ang = positions.astype(F32)[..., None] * inv_freq
    cos = _rope_lanes(jnp.concatenate([jnp.cos(ang), jnp.cos(ang)], axis=-1))
    sin = _rope_lanes(jnp.concatenate([-jnp.sin(ang), jnp.sin(ang)], axis=-1))

    km, vm = _memkv(mem, norm_mem, w_ca_kv.astype(BF16), ca_k_norm)
    gla_consts = _rec_consts(GLA_DK, GLA_DV)
    hgrn_consts = _rec_consts(HGRN_DK, HGRN_DV)
    row = lambda a: a.reshape(1, -1)
    n_a = HEADS * MLA_V
    n_b = HEADS * GLA_DV

    h = x
    for l in range(depth):
        w_l = _relayout_w_in(w_in[l]).astype(BF16)
        wq_l = _qk_lanes(w_q_up[l].reshape(MLA_Q_LORA, HEADS, MLA_QK)).reshape(MLA_Q_LORA, -1).astype(BF16)
        q, k, v, pb, pc = _inproj(
            h, row(norm_mix[l]), w_l, row(mla_q_lat_norm[l]), wq_l, row(mla_kv_lat_norm[l]),
            w_kv_up[l].astype(BF16), row(_qk_lanes(mla_q_norm[l])), row(_qk_lanes(mla_k_norm[l])),
            cos, sin, tm)
        ya = _mla_attn(q, k, v, t_attn)
        wg_l = jnp.pad(gla_w_gate[l], ((0, LANES - GLA_RANK), (0, 0))).astype(BF16)
        yb = _recurrence(_gla_kernel, "gla", pb,
                         (wg_l, row(gla_b_gate[l]), row(jnp.tile(gla_out_norm[l], HEADS))),
                         gla_consts, GLA_DK, GLA_DV, ts_rec)
        yc = _recurrence(_hgrn_kernel, "hgrn", pc,
                         (row(lower_bounds[l]), row(jnp.tile(hgrn_out_norm[l], HEADS))),
                         hgrn_consts, HGRN_DK, HGRN_DV, ts_rec)
        wm = w_mix_out[l].astype(BF16)
        h = _mixca(h, ya, yb, yc, wm[:n_a], wm[n_a:n_a + n_b], wm[n_a + n_b:], row(norm_ca[l]),
                   w_ca_q[l].astype(BF16), row(ca_q_norm[l]), km[l], vm[l], w_ca_out[l].astype(BF16), tm)
        wf = w_ffn_in[l].astype(BF16)
        cw = jnp.pad(ffn_conv_w[l], ((0, 8 - ffn_conv_w.shape[1]), (0, 0)))
        h = _ffn(h, row(norm_ffn[l]), wf[:, :d_ff], wf[:, d_ff:], cw, row(ffn_conv_b[l]),
                 w_ffn_out[l].astype(BF16), tm, fc)
    return h
```

```python
import functools

import numpy as np
import jax
import jax.numpy as jnp
from jax import lax
from jax.experimental import pallas as pl
from jax.experimental.pallas import tpu as pltpu

F32 = jnp.float32
BF16 = jnp.bfloat16

EPS = 1e-6
MASK_VALUE = -1e30
LB_FLOOR = 1e-30
CHUNK = 64
ROPE_THETA = 10000.0
LOG2_E = 1.4426950408889634

HEADS = 4
MLA_Q_LORA = 256
MLA_KV_LORA = 256
MLA_NOPE = 128
MLA_ROPE = 64
MLA_V = 128
MLA_QK = MLA_NOPE + MLA_ROPE
MLA_QK_PAD = 256
GLA_DK = 32
GLA_DV = 64
GLA_RANK = 16
GLA_TAU = 16.0
HGRN_DK = 64
HGRN_DV = 64
CA_DIM = 128
MLA_Q_COLS = 384
N_A = 768
N_B = 896
N_C = 1024

LANES = 128
MXU_COLS = 256
REC_CHUNK = 128
REC_BLOCK = 16
VMEM_LIMIT = 56 * 1024 * 1024

NT_DIMS = (((1,), (1,)), ((), ()))
TN_DIMS = (((0,), (0,)), ((), ()))


def _dot(a, b):
    return jnp.dot(a, b, preferred_element_type=F32)


def _dot_nt(a, b):
    return lax.dot_general(a, b, NT_DIMS, preferred_element_type=F32)


def _rms(x, gain, dim=None):
    dim = x.shape[-1] if dim is None else dim
    ms = jnp.sum(x * x, axis=-1, keepdims=True) * (1.0 / dim)
    return x * lax.rsqrt(ms + EPS) * gain


def _sigmoid(x):
    return 1.0 / (1.0 + jnp.exp(-x))


def _log_sigmoid(x):
    return jnp.minimum(x, 0.0) - jnp.log1p(jnp.exp(-jnp.abs(x)))


def _params(*sem):
    return pltpu.CompilerParams(dimension_semantics=sem, vmem_limit_bytes=VMEM_LIMIT)


def _const_spec(shape):
    nd = len(shape)
    return pl.BlockSpec(shape, lambda *_: (0,) * nd)


def _memkv_kernel(mem_ref, g_ref, w_ref, gk_ref, k_ref, v_ref):
    n = _rms(mem_ref[0], g_ref[0]).astype(BF16)
    kv = _dot(n, w_ref[0])
    for h in range(HEADS):
        base = 2 * CA_DIM * h
        k_ref[0, 0, h] = _rms(kv[:, base:base + CA_DIM], gk_ref[0]).astype(BF16)
        v_ref[0, 0, h] = kv[:, base + CA_DIM:base + 2 * CA_DIM].astype(BF16)


def _memkv(mem, norm_mem, w_ca_kv, ca_k_norm):
    depth, d = norm_mem.shape
    b, m, _ = mem.shape
    out = jax.ShapeDtypeStruct((depth, b, HEADS, m, CA_DIM), BF16)
    spec_o = pl.BlockSpec((1, 1, HEADS, m, CA_DIM), lambda l, i: (l, i, 0, 0, 0))
    return pl.pallas_call(
        _memkv_kernel,
        grid=(depth, b),
        in_specs=[
            pl.BlockSpec((1, m, d), lambda l, i: (i, 0, 0)),
            pl.BlockSpec((1, 1, d), lambda l, i: (l, 0, 0)),
            pl.BlockSpec((1, d, w_ca_kv.shape[-1]), lambda l, i: (l, 0, 0)),
            pl.BlockSpec((1, 1, CA_DIM), lambda l, i: (l, 0, 0)),
        ],
        out_specs=[spec_o, spec_o],
        out_shape=[out, out],
        compiler_params=_params("arbitrary", "arbitrary"),
        name="memkv",
    )(mem, norm_mem.reshape(depth, 1, d), w_ca_kv, ca_k_norm.reshape(depth, 1, CA_DIM))


def _inproj_kernel(h_ref, g_ref, w_ref, gql_ref, wq_ref, gkvl_ref, wkv_ref, gq_ref, gk_ref,
                   cos_ref, sin_ref, ones_ref, bd_ref, q_ref, k_ref, v_ref, pb_ref, pc_ref):
    n = _rms(h_ref[0], g_ref[...]).astype(BF16)
    pa = _dot(n, w_ref[:, :N_A])
    pb_ref[0] = _dot(n, w_ref[:, N_A:N_A + N_B])
    pc_ref[0] = _dot(n, w_ref[:, N_A + N_B:])

    def latent(x, gain):
        ms = _dot((x * x).astype(BF16), ones_ref[...]) * (1.0 / x.shape[1])
        return (x * lax.rsqrt(ms + EPS) * gain).astype(BF16)

    qf = _dot(latent(pa[:, :MLA_Q_LORA], gql_ref[...]), wq_ref[...])
    kvf = _dot(latent(pa[:, MLA_Q_LORA:MLA_Q_LORA + MLA_KV_LORA], gkvl_ref[...]), wkv_ref[...])
    kr = pa[:, N_A - 2 * LANES:N_A - LANES]
    kr_sw = pa[:, N_A - LANES:]
    cos = cos_ref[0]
    sin = sin_ref[0]
    gq = gq_ref[...]
    gk = gk_ref[...]
    scale = MLA_QK ** -0.5 * LOG2_E
    gq_nope = gq[:, :LANES] * scale
    q_cos = gq[:, LANES:2 * LANES] * cos * scale
    q_sin = gq[:, 2 * LANES:] * sin * scale
    k_rot = kr * (gk[:, LANES:2 * LANES] * cos) + kr_sw * (gk[:, 2 * LANES:] * sin)
    kr_sq = kr * kr
    q_sq = jnp.concatenate(
        [qf[:, MLA_Q_COLS * h:MLA_Q_COLS * h + LANES] ** 2
         + qf[:, MLA_Q_COLS * h + LANES:MLA_Q_COLS * h + 2 * LANES] ** 2 for h in range(HEADS)], axis=1)
    k_sq = jnp.concatenate(
        [kvf[:, 2 * LANES * h:2 * LANES * h + LANES] ** 2 + kr_sq for h in range(HEADS)], axis=1)
    q_rs = lax.rsqrt(_dot(q_sq.astype(BF16), bd_ref[...]) * (1.0 / MLA_QK) + EPS)
    k_rs = lax.rsqrt(_dot(k_sq.astype(BF16), bd_ref[...]) * (1.0 / MLA_QK) + EPS)
    for h in range(HEADS):
        qb = MLA_Q_COLS * h
        rs = q_rs[:, LANES * h:LANES * (h + 1)]
        q_ref[0, h, :, :MLA_NOPE] = (qf[:, qb:qb + LANES] * rs * gq_nope).astype(BF16)
        q_ref[0, h, :, MLA_NOPE:] = ((qf[:, qb + LANES:qb + 2 * LANES] * q_cos
                                      + qf[:, qb + 2 * LANES:qb + 3 * LANES] * q_sin) * rs).astype(BF16)
        kb = 2 * LANES * h
        rs = k_rs[:, LANES * h:LANES * (h + 1)]
        k_ref[0, h, :, :MLA_NOPE] = (kvf[:, kb:kb + LANES] * rs * gk[:, :LANES]).astype(BF16)
        k_ref[0, h, :, MLA_NOPE:] = (k_rot * rs).astype(BF16)
        v_ref[0, h] = kvf[:, kb + LANES:kb + 2 * LANES].astype(BF16)


def _inproj(h, g, w, gql, wq, gkvl, wkv, gq, gk, cos, sin, tm):
    ones = jnp.ones((MLA_Q_LORA, MLA_Q_LORA), BF16)
    head_of = np.arange(HEADS * LANES) // LANES
    bd = jnp.asarray(head_of[:, None] == head_of[None, :], BF16)
    b, s, d = h.shape
    grid = (b, s // tm)
    tok = lambda last: pl.BlockSpec((1, tm, last), lambda i, t: (i, t, 0))
    head = lambda last: pl.BlockSpec((1, HEADS, tm, last), lambda i, t: (i, 0, t, 0))
    return pl.pallas_call(
        _inproj_kernel,
        grid=grid,
        in_specs=[tok(d), _const_spec(g.shape), _const_spec(w.shape), _const_spec(gql.shape),
                  _const_spec(wq.shape), _const_spec(gkvl.shape), _const_spec(wkv.shape),
                  _const_spec(gq.shape), _const_spec(gk.shape), tok(LANES), tok(LANES),
                  _const_spec(ones.shape), _const_spec(bd.shape)],
        out_specs=[head(MLA_QK_PAD), head(MLA_QK_PAD), head(MLA_V), tok(N_B), tok(N_C)],
        out_shape=[jax.ShapeDtypeStruct((b, HEADS, s, MLA_QK_PAD), BF16),
                   jax.ShapeDtypeStruct((b, HEADS, s, MLA_QK_PAD), BF16),
                   jax.ShapeDtypeStruct((b, HEADS, s, MLA_V), BF16),
                   jax.ShapeDtypeStruct((b, s, N_B), F32),
                   jax.ShapeDtypeStruct((b, s, N_C), F32)],
        compiler_params=_params("arbitrary", "arbitrary"),
        name="inproj",
    )(h, g, w, gql, wq, gkvl, wkv, gq, gk, cos, sin, ones, bd)


def _mla_attn_kernel(qa_ref, qb_ref, k_ref, v_ref, o_ref, q_sc, m_sc, l_sc, acc_sc, sa_sc, sb_sc, *, t, nq):
    g = pl.program_id(2)
    q_sc[0:t, :] = qa_ref[0, 0]
    q_sc[t:2 * t, :] = qb_ref[0, 0]
    m_sc[...] = jnp.full(m_sc.shape, MASK_VALUE, F32)
    l_sc[...] = jnp.zeros(l_sc.shape, F32)
    acc_sc[...] = jnp.zeros(acc_sc.shape, F32)

    def scores(rows, kt):
        return _dot_nt(q_sc[rows, :], k_ref[0, 0, pl.ds(kt * t, t), :])

    def update(s, rows, kt):
        m_prev = m_sc[rows, :]
        m_new = jnp.maximum(m_prev, jnp.max(s, axis=-1, keepdims=True))
        alpha = jnp.exp2(m_prev - m_new)
        p = jnp.exp2(s - jnp.concatenate([m_new] * (t // LANES), axis=1))
        l_sc[rows, :] = alpha * l_sc[rows, :] + jnp.sum(p, axis=-1, keepdims=True)
        v_tile = v_ref[0, 0, pl.ds(kt * t, t), :]
        acc_sc[rows, :] = alpha * acc_sc[rows, :] + _dot(p.astype(BF16), v_tile)
        m_sc[rows, :] = m_new

    qc = lax.broadcasted_iota(jnp.int32, (t, t), 0) // CHUNK
    kc = lax.broadcasted_iota(jnp.int32, (t, t), 1) // CHUNK
    bufs = (sa_sc, sb_sc)
    rows_a, rows_b = pl.ds(0, t), pl.ds(t, t)

    def run(ga):
        def body():
            work = ([(rows_a, j, False) for j in range(ga)]
                    + [(rows_b, j, False) for j in range(nq - 1 - ga)]
                    + [(rows_a, ga, True), (rows_b, nq - 1 - ga, True)])
            sa_sc[...] = scores(work[0][0], work[0][1])
            for i, (rows, kt, diagonal) in enumerate(work):
                if i + 1 < len(work):
                    bufs[(i + 1) % 2][...] = scores(work[i + 1][0], work[i + 1][1])
                s = bufs[i % 2][...]
                if diagonal:
                    s = jnp.where(kc <= qc, s, MASK_VALUE)
                update(s, rows, kt)
        return body

    for ga in range(nq // 2):
        pl.when(g == ga)(run(ga))

    o = (acc_sc[...] / l_sc[...]).astype(BF16)
    o_ref[0, 0, 0] = o[:t]
    o_ref[0, 1, 0] = o[t:]


def _mla_attn(q, k, v, t):
    b, nh, s, dq = q.shape
    dv = v.shape[-1]
    nq = s // t
    half = nq // 2
    out = pl.pallas_call(
        functools.partial(_mla_attn_kernel, t=t, nq=nq),
        grid=(b, nh, half),
        in_specs=[pl.BlockSpec((1, 1, t, dq), lambda i, h, j: (i, h, j, 0)),
                  pl.BlockSpec((1, 1, t, dq), lambda i, h, j: (i, h, nq - 1 - j, 0)),
                  pl.BlockSpec((1, 1, s, dq), lambda i, h, j: (i, h, 0, 0)),
                  pl.BlockSpec((1, 1, s, dv), lambda i, h, j: (i, h, 0, 0))],
        out_specs=pl.BlockSpec((1, 2, 1, t, dv), lambda i, h, j: (i, 0, j, 0, h)),
        out_shape=jax.ShapeDtypeStruct((b, 2, half, t, nh * dv), BF16),
        scratch_shapes=[pltpu.VMEM((2 * t, dq), BF16), pltpu.VMEM((2 * t, LANES), F32),
                        pltpu.VMEM((2 * t, LANES), F32), pltpu.VMEM((2 * t, dv), F32),
                        pltpu.VMEM((t, t), F32), pltpu.VMEM((t, t), F32)],
        compiler_params=_params("arbitrary", "arbitrary", "arbitrary"),
        name="mla_attn",
    )(q, q, k, v)
    return jnp.concatenate([out[:, 0], out[:, 1, ::-1]], axis=1).reshape(b, s, nh * dv)


def _rec_consts(dk, dv):
    hk, hv = HEADS * dk, HEADS * dv
    c, blk = REC_CHUNK, REC_BLOCK
    head_k = np.arange(hk) // dk
    head_v = np.arange(hv) // dv
    head_r = np.repeat(np.arange(HEADS), blk)
    ltri = (np.arange(c)[:, None] >= np.arange(c)[None, :])
    ind = head_k[:, None] == head_v[None, :]
    col = (head_k[None, :] * blk + np.arange(blk)[:, None]).reshape(-1)
    ind_diag = col[:, None] == np.arange(HEADS * blk)[None, :]
    indv = (head_v[:, None] == head_v[None, :]) / dv
    hm_k = head_r[:, None] == head_k[None, :]
    hm_v = head_r[:, None] == head_v[None, :]
    return (jnp.asarray(ltri, BF16), jnp.asarray(ind_diag, BF16), jnp.asarray(indv, BF16),
            jnp.asarray(hm_k, F32), jnp.asarray(hm_v, F32), jnp.asarray(ind.T, F32))


def _rec_tile(q, k, v, g, st_ref, plo_sc, phi_sc, o_sc, ltri_ref, ind_ref, hmk_ref, hmv_ref, hms_ref):
    ts, hk = q.shape
    hv = v.shape[1]
    c, blk = REC_CHUNK, REC_BLOCK
    nb = c // blk
    half = blk // 2
    chunks = [slice(i * c, (i + 1) * c) for i in range(ts // c)]
    g2 = g * LOG2_E
    g_hi = g2.astype(BF16)
    g_lo = (g2 - g_hi.astype(F32)).astype(BF16)
    ltri = ltri_ref[...]
    b = [_dot(ltri, g_hi[r]) + _dot(ltri, g_lo[r]) for r in chunks]

    sub = lax.broadcasted_iota(jnp.int32, (nb, half, hk), 1)
    diag = []
    for ci, r in enumerate(chunks):
        b4 = b[ci].reshape(nb, 2, half, hk)
        q4 = q[r].reshape(nb, 2, half, hk)
        k4 = k[r].reshape(nb, 2, half, hk)
        for jj in range(blk):
            hf, jr = divmod(jj, half)
            bj = b4[:, hf, jr:jr + 1, :]
            kj = k4[:, hf, jr:jr + 1, :]
            p = jnp.where(sub >= jr, q4[:, hf] * kj * jnp.exp2(b4[:, hf] - bj), 0.0)
            if hf == 0:
                plo_sc[ci, :, jj * hk:(jj + 1) * hk] = p.reshape(c // 2, hk).astype(BF16)
                p = q4[:, 1] * kj * jnp.exp2(b4[:, 1] - bj)
            phi_sc[ci, :, jj * hk:(jj + 1) * hk] = p.reshape(c // 2, hk).astype(BF16)
        d_lo = _dot(plo_sc[ci], ind_ref[:half * hk, :]).reshape(nb, half, HEADS * blk)
        d_hi = _dot(phi_sc[ci], ind_ref[...]).reshape(nb, half, HEADS * blk)
        diag.append(jnp.stack([d_lo, d_hi], axis=1).reshape(c, HEADS * blk))

    hmk = hmk_ref[...]
    hmv = hmv_ref[...]
    for jb in range(nb):
        lo, hi = blk * jb, blk * (jb + 1)
        for ci, r in enumerate(chunks):
            qc, kc, vc, bc = q[r], k[r], v[r], b[ci]
            a = diag[ci][lo:hi, :]
            if hi < c:
                b_end = bc[hi - 1:hi, :]
                qj = (qc[hi:, :] * jnp.exp2(bc[hi:, :] - b_end)).astype(BF16)
                kj = kc[lo:hi, :] * jnp.exp2(b_end - bc[lo:hi, :])
                kbd = (jnp.concatenate([kj] * HEADS, axis=0) * hmk).astype(BF16)
                a = jnp.concatenate([a, _dot_nt(qj, kbd)], axis=0)
            vbd = (jnp.concatenate([vc[lo:hi, :]] * HEADS, axis=0) * hmv).astype(BF16)
            contrib = _dot(a.astype(BF16), vbd)
            if jb == 0:
                o_sc[r, :] = contrib
            else:
                o_sc[r.start + lo:r.stop, :] += contrib

    upd = []
    for ci, r in enumerate(chunks):
        kh = (k[r] * jnp.exp2(b[ci][c - 1:c, :] - b[ci])).astype(BF16)
        upd.append(lax.dot_general(v[r].astype(BF16), kh, TN_DIMS, preferred_element_type=F32))
    st = st_ref[...]
    for ci, r in enumerate(chunks):
        o_sc[r, :] += _dot_nt((q[r] * jnp.exp2(b[ci])).astype(BF16), st.astype(BF16))
        st = st * jnp.exp2(b[ci][c - 1:c, :]) + upd[ci] * hms_ref[...]
    st_ref[...] = st


def _head_rms_gate(o, gate, gain, indv_ref):
    ms = _dot((o * o).astype(BF16), indv_ref[...])
    return o * lax.rsqrt(ms + EPS) * gain * (gate * _sigmoid(gate))


def _gla_kernel(pb_ref, wg_ref, bg_ref, gn_ref, ltri_ref, ind_ref, indv_ref, hmk_ref, hmv_ref, hms_ref,
                y_ref, st_ref, plo_sc, phi_sc, o_sc):
    @pl.when(pl.program_id(1) == 0)
    def _():
        st_ref[...] = jnp.zeros(st_ref.shape, F32)

    hk, hv = HEADS * GLA_DK, HEADS * GLA_DV
    x = pb_ref[0]
    q = x[:, :hk] * (GLA_DK ** -0.5)
    k = x[:, hk:2 * hk]
    v = x[:, 2 * hk:2 * hk + hv]
    gate = x[:, 2 * hk + hv:2 * hk + 2 * hv]
    logits = _dot(x[:, 2 * hk + 2 * hv:].astype(BF16), wg_ref[...]) + bg_ref[...]
    g = _log_sigmoid(logits) * (1.0 / GLA_TAU)
    _rec_tile(q, k, v, g, st_ref, plo_sc, phi_sc, o_sc, ltri_ref, ind_ref, hmk_ref, hmv_ref, hms_ref)
    y_ref[0] = _head_rms_gate(o_sc[...], gate, gn_ref[...], indv_ref).astype(BF16)


def _hgrn_kernel(pc_ref, lb_ref, gn_ref, ltri_ref, ind_ref, indv_ref, hmk_ref, hmv_ref, hms_ref,
                 y_ref, st_ref, plo_sc, phi_sc, o_sc):
    @pl.when(pl.program_id(1) == 0)
    def _():
        st_ref[...] = jnp.zeros(st_ref.shape, F32)

    hk, hv = HEADS * HGRN_DK, HEADS * HGRN_DV
    x = pc_ref[0]
    cq = x[:, :hk]
    z = x[:, hk:2 * hk]
    v = x[:, 2 * hk:2 * hk + hv]
    gate = x[:, 2 * hk + hv:]
    lb = lb_ref[...]
    f = jnp.maximum(lb, LB_FLOOR) + (1.0 - lb) * _sigmoid(z)
    g = jnp.minimum(jnp.log(f), 0.0)
    k = (1.0 - lb) * _sigmoid(-z)
    q = cq * _sigmoid(cq) * (HGRN_DK ** -0.5)
    _rec_tile(q, k, v, g, st_ref, plo_sc, phi_sc, o_sc, ltri_ref, ind_ref, hmk_ref, hmv_ref, hms_ref)
    y_ref[0] = _head_rms_gate(o_sc[...], gate, gn_ref[...], indv_ref).astype(BF16)


def _recurrence(kernel, name, x, small, consts, dk, dv, ts):
    b, s, width = x.shape
    hk, hv = HEADS * dk, HEADS * dv
    ltri, ind, indv, hmk, hmv, hms = consts
    operands = (x, *small, ltri, ind, indv, hmk, hmv, hms)
    in_specs = [pl.BlockSpec((1, ts, width), lambda i, t: (i, t, 0))]
    in_specs += [_const_spec(a.shape) for a in operands[1:]]
    return pl.pallas_call(
        kernel,
        grid=(b, s // ts),
        in_specs=in_specs,
        out_specs=pl.BlockSpec((1, ts, hv), lambda i, t: (i, t, 0)),
        out_shape=jax.ShapeDtypeStruct((b, s, hv), BF16),
        scratch_shapes=[pltpu.VMEM((hv, hk), F32),
                        pltpu.VMEM((ts // REC_CHUNK, REC_CHUNK // 2, REC_BLOCK // 2 * hk), BF16),
                        pltpu.VMEM((ts // REC_CHUNK, REC_CHUNK // 2, REC_BLOCK * hk), BF16),
                        pltpu.VMEM((ts, hv), F32)],
        compiler_params=_params("arbitrary", "arbitrary"),
        name=name,
    )(*operands)


def _mixca_kernel(h_ref, ya_ref, yb_ref, yc_ref, wa_ref, wb_ref, wc_ref, g_ref, wq_ref, gq_ref,
                  km_ref, vm_ref, wo_ref, out_ref):
    h1 = (h_ref[0] + _dot(ya_ref[0], wa_ref[...]) + _dot(yb_ref[0], wb_ref[...])
          + _dot(yc_ref[0], wc_ref[...]))
    qc = _dot(_rms(h1, g_ref[...]).astype(BF16), wq_ref[...])
    heads = []
    for h in range(HEADS):
        qn = _rms(qc[:, CA_DIM * h:CA_DIM * (h + 1)], gq_ref[...]) * (CA_DIM ** -0.5)
        s = _dot_nt(qn.astype(BF16), km_ref[0, h])
        p = jnp.exp(s - jnp.max(s, axis=-1, keepdims=True))
        o = _dot(p.astype(BF16), vm_ref[0, h]) / jnp.sum(p, axis=-1, keepdims=True)
        heads.append(o.astype(BF16))
    out_ref[0] = h1 + _dot(jnp.concatenate(heads, axis=-1), wo_ref[...])


def _mixca(h, ya, yb, yc, wa, wb, wc, g, wq, gq, km, vm, wo, tm):
    b, s, d = h.shape
    tok = lambda last: pl.BlockSpec((1, tm, last), lambda i, t: (i, t, 0))
    mem = pl.BlockSpec((1,) + km.shape[1:], lambda i, t: (i, 0, 0, 0))
    return pl.pallas_call(
        _mixca_kernel,
        grid=(b, s // tm),
        in_specs=[tok(d), tok(ya.shape[-1]), tok(yb.shape[-1]), tok(yc.shape[-1]),
                  _const_spec(wa.shape), _const_spec(wb.shape), _const_spec(wc.shape),
                  _const_spec(g.shape), _const_spec(wq.shape), _const_spec(gq.shape),
                  mem, mem, _const_spec(wo.shape)],
        out_specs=tok(d),
        out_shape=jax.ShapeDtypeStruct((b, s, d), F32),
        compiler_params=_params("arbitrary", "arbitrary"),
        name="mix_ca",
    )(h, ya, yb, yc, wa, wb, wc, g, wq, gq, km, vm, wo)


def _ffn_kernel(h_ref, g_ref, wg_ref, wu_ref, cw_ref, cb_ref, wo_ref, out_ref, gs_sc, carry_sc, *, bounds):
    tm = h_ref.shape[1]

    @pl.when(pl.program_id(1) == 0)
    def _():
        carry_sc[...] = jnp.zeros(carry_sc.shape, F32)

    x = h_ref[0]
    n = _rms(x, g_ref[...]).astype(BF16)
    acc = x
    for c0, c1 in zip(bounds[:-1], bounds[1:]):
        fc = c1 - c0
        gate = _dot(n, wg_ref[:, c0:c1])
        gs_sc[0:8, :fc] = carry_sc[:, c0:c1]
        gs_sc[8:8 + tm, :fc] = gate
        carry_sc[:, c0:c1] = gate[tm - 8:, :]
        conv = (cb_ref[:, c0:c1] + gs_sc[6:6 + tm, :fc] * cw_ref[0:1, c0:c1]
                + gs_sc[7:7 + tm, :fc] * cw_ref[1:2, c0:c1] + gate * cw_ref[2:3, c0:c1])
        up = _dot(n, wu_ref[:, c0:c1])
        act = (conv * _sigmoid(conv) * up).astype(BF16)
        acc = acc + _dot(act, wo_ref[c0:c1, :])
    out_ref[0] = acc


def _ffn(h, g, wg, wu, cw, cb, wo, tm):
    b, s, d = h.shape
    d_ff = wg.shape[1]
    n_tiles = -(-d_ff // MXU_COLS)
    bounds = (0, min(d_ff, (n_tiles + 1) // 2 * MXU_COLS), d_ff)
    fc = max(bounds[1], d_ff - bounds[1])
    tok = pl.BlockSpec((1, tm, d), lambda i, t: (i, t, 0))
    return pl.pallas_call(
        functools.partial(_ffn_kernel, bounds=bounds),
        grid=(b, s // tm),
        in_specs=[tok, _const_spec(g.shape), _const_spec(wg.shape), _const_spec(wu.shape),
                  _const_spec(cw.shape), _const_spec(cb.shape), _const_spec(wo.shape)],
        out_specs=tok,
        out_shape=jax.ShapeDtypeStruct((b, s, d), F32),
        scratch_shapes=[pltpu.VMEM((tm + 8, fc), F32), pltpu.VMEM((8, d_ff), F32)],
        compiler_params=_params("arbitrary", "arbitrary"),
        name="ffn",
    )(h, g, wg, wu, cw, cb, wo)


def _rope_lanes(x):
    half = MLA_ROPE // 2
    z = jnp.zeros(x.shape[:-1] + (LANES // 2 - half,), x.dtype)
    return jnp.concatenate([x[..., :half], z, x[..., half:], z], axis=-1)


def _rope_lanes_swapped(x):
    half = MLA_ROPE // 2
    return _rope_lanes(jnp.concatenate([x[..., half:], x[..., :half]], axis=-1))


def _qk_lanes(x):
    rope = x[..., MLA_NOPE:]
    return jnp.concatenate([x[..., :MLA_NOPE], _rope_lanes(rope), _rope_lanes_swapped(rope)], axis=-1)


def _relayout_w_in(w):
    offs = np.cumsum([0, MLA_Q_LORA, MLA_KV_LORA, MLA_ROPE, HEADS * GLA_DK, HEADS * GLA_DK, HEADS * GLA_DV,
                      GLA_RANK, HEADS * GLA_DV, HEADS * HGRN_DK, HEADS * HGRN_DK, HEADS * HGRN_DV,
                      HEADS * HGRN_DV])
    seg = lambda i: w[:, offs[i]:offs[i + 1]]
    gate_lr = jnp.pad(seg(6), ((0, 0), (0, LANES - GLA_RANK)))
    return jnp.concatenate([seg(0), seg(1), _rope_lanes(seg(2)), _rope_lanes_swapped(seg(2)),
                            seg(3), seg(4), seg(5), seg(7), gate_lr,
                            seg(8), seg(9), seg(10), seg(11)], axis=-1)


def _pick(n, pref):
    return pref if n % pref == 0 else n


def kernel(x, mem, positions, norm_mix, w_in, mla_q_lat_norm, w_q_up, mla_kv_lat_norm, w_kv_up, mla_q_norm,
           mla_k_norm, gla_w_gate, gla_b_gate, gla_out_norm, hgrn_lb_logits, hgrn_out_norm, w_mix_out,
           norm_ca, norm_mem, w_ca_q, w_ca_kv, ca_q_norm, ca_k_norm, w_ca_out, norm_ffn, w_ffn_in,
           ffn_conv_w, ffn_conv_b, w_ffn_out):
    b, s, d = x.shape
    depth = w_in.shape[0]
    d_ff = w_ffn_out.shape[1]
    tm = _pick(s, 512)
    t_attn = _pick(s, 512)
    ts_rec = _pick(s, 512)

    gamma = jax.nn.softmax(hgrn_lb_logits.astype(F32), axis=0)
    lower_bounds = jnp.cumsum(gamma, axis=0) - gamma[0]
    inv_freq = ROPE_THETA ** (-jnp.arange(0, MLA_ROPE, 2, dtype=F32) / MLA_ROPE)
    ang = positions.astype(F32)[..., None] * inv_freq
    cos = _rope_lanes(jnp.concatenate([jnp.cos(ang), jnp.cos(ang)], axis=-1))
    sin = _rope_lanes(jnp.concatenate([-jnp.sin(ang), jnp.sin(ang)], axis=-1))

    km, vm = _memkv(mem, norm_mem, w_ca_kv.astype(BF16), ca_k_norm)
    gla_consts = _rec_consts(GLA_DK, GLA_DV)
    hgrn_consts = _rec_consts(HGRN_DK, HGRN_DV)
    row = lambda a: a.reshape(1, -1)
    n_a = HEADS * MLA_V
    n_b = HEADS * GLA_DV

    h = x
    for l in range(depth):
        w_l = _relayout_w_in(w_in[l]).astype(BF16)
        wq_l = _qk_lanes(w_q_up[l].reshape(MLA_Q_LORA, HEADS, MLA_QK)).reshape(MLA_Q_LORA, -1).astype(BF16)
        q, k, v, pb, pc = _inproj(
            h, row(norm_mix[l]), w_l, row(mla_q_lat_norm[l]), wq_l, row(mla_kv_lat_norm[l]),
            w_kv_up[l].astype(BF16), row(_qk_lanes(mla_q_norm[l])), row(_qk_lanes(mla_k_norm[l])),
            cos, sin, tm)
        ya = _mla_attn(q, k, v, t_attn)
        wg_l = jnp.pad(gla_w_gate[l], ((0, LANES - GLA_RANK), (0, 0))).astype(BF16)
        yb = _recurrence(_gla_kernel, "gla", pb,
                         (wg_l, row(gla_b_gate[l]), row(jnp.tile(gla_out_norm[l], HEADS))),
                         gla_consts, GLA_DK, GLA_DV, ts_rec)
        yc = _recurrence(_hgrn_kernel, "hgrn", pc,
                         (row(lower_bounds[l]), row(jnp.tile(hgrn_out_norm[l], HEADS))),
                         hgrn_consts, HGRN_DK, HGRN_DV, ts_rec)
        wm = w_mix_out[l].astype(BF16)
        h = _mixca(h, ya, yb, yc, wm[:n_a], wm[n_a:n_a + n_b], wm[n_a + n_b:], row(norm_ca[l]),
                   w_ca_q[l].astype(BF16), row(ca_q_norm[l]), km[l], vm[l], w_ca_out[l].astype(BF16), tm)
        wf = w_ffn_in[l].astype(BF16)
        cw = jnp.pad(ffn_conv_w[l], ((0, 8 - ffn_conv_w.shape[1]), (0, 0)))
        h = _ffn(h, row(norm_ffn[l]), wf[:, :d_ff], wf[:, d_ff:], cw, row(ffn_conv_b[l]),
                 w_ffn_out[l].astype(BF16), tm)
    return h
```

```python
import functools

import numpy as np
import jax
import jax.numpy as jnp
from jax import lax
from jax.experimental import pallas as pl
from jax.experimental.pallas import tpu as pltpu

F32 = jnp.float32
BF16 = jnp.bfloat16

EPS = 1e-6
MASK_VALUE = -1e30
LB_FLOOR = 1e-30
CHUNK = 64
ROPE_THETA = 10000.0
LOG2_E = 1.4426950408889634

HEADS = 4
MLA_Q_LORA = 256
MLA_KV_LORA = 256
MLA_NOPE = 128
MLA_ROPE = 64
MLA_V = 128
MLA_QK = MLA_NOPE + MLA_ROPE
MLA_QK_PAD = 256
GLA_DK = 32
GLA_DV = 64
GLA_RANK = 16
GLA_TAU = 16.0
HGRN_DK = 64
HGRN_DV = 64
CA_DIM = 128
MLA_Q_COLS = 384
N_A = 768
N_B = 896
N_C = 1024

LANES = 128
MXU_COLS = 256
REC_CHUNK = 128
REC_BLOCK = 16
VMEM_LIMIT = 56 * 1024 * 1024

NT_DIMS = (((1,), (1,)), ((), ()))
TN_DIMS = (((0,), (0,)), ((), ()))


def _dot(a, b):
    return jnp.dot(a, b, preferred_element_type=F32)


def _dot_nt(a, b):
    return lax.dot_general(a, b, NT_DIMS, preferred_element_type=F32)


def _rms(x, gain, dim=None):
    dim = x.shape[-1] if dim is None else dim
    ms = jnp.sum(x * x, axis=-1, keepdims=True) * (1.0 / dim)
    return x * lax.rsqrt(ms + EPS) * gain


def _sigmoid(x):
    return 1.0 / (1.0 + jnp.exp(-x))


def _log_sigmoid(x):
    return jnp.minimum(x, 0.0) - jnp.log1p(jnp.exp(-jnp.abs(x)))


def _params(*sem):
    return pltpu.CompilerParams(dimension_semantics=sem, vmem_limit_bytes=VMEM_LIMIT)


def _const_spec(shape):
    nd = len(shape)
    return pl.BlockSpec(shape, lambda *_: (0,) * nd)


def _memkv_kernel(mem_ref, g_ref, w_ref, gk_ref, k_ref, v_ref):
    n = _rms(mem_ref[0], g_ref[0]).astype(BF16)
    kv = _dot(n, w_ref[0])
    for h in range(HEADS):
        base = 2 * CA_DIM * h
        k_ref[0, 0, h] = _rms(kv[:, base:base + CA_DIM], gk_ref[0]).astype(BF16)
        v_ref[0, 0, h] = kv[:, base + CA_DIM:base + 2 * CA_DIM].astype(BF16)


def _memkv(mem, norm_mem, w_ca_kv, ca_k_norm):
    depth, d = norm_mem.shape
    b, m, _ = mem.shape
    out = jax.ShapeDtypeStruct((depth, b, HEADS, m, CA_DIM), BF16)
    spec_o = pl.BlockSpec((1, 1, HEADS, m, CA_DIM), lambda l, i: (l, i, 0, 0, 0))
    return pl.pallas_call(
        _memkv_kernel,
        grid=(depth, b),
        in_specs=[
            pl.BlockSpec((1, m, d), lambda l, i: (i, 0, 0)),
            pl.BlockSpec((1, 1, d), lambda l, i: (l, 0, 0)),
            pl.BlockSpec((1, d, w_ca_kv.shape[-1]), lambda l, i: (l, 0, 0)),
            pl.BlockSpec((1, 1, CA_DIM), lambda l, i: (l, 0, 0)),
        ],
        out_specs=[spec_o, spec_o],
        out_shape=[out, out],
        compiler_params=_params("arbitrary", "arbitrary"),
        name="memkv",
    )(mem, norm_mem.reshape(depth, 1, d), w_ca_kv, ca_k_norm.reshape(depth, 1, CA_DIM))


def _inproj_kernel(h_ref, g_ref, w_ref, gql_ref, wq_ref, gkvl_ref, wkv_ref, gq_ref, gk_ref,
                   cos_ref, sin_ref, ones_ref, bd_ref, q_ref, k_ref, v_ref, pb_ref, pc_ref):
    n = _rms(h_ref[0], g_ref[...]).astype(BF16)
    pa = _dot(n, w_ref[:, :N_A])
    pb_ref[0] = _dot(n, w_ref[:, N_A:N_A + N_B])
    pc_ref[0] = _dot(n, w_ref[:, N_A + N_B:])

    def latent(x, gain):
        ms = _dot((x * x).astype(BF16), ones_ref[...]) * (1.0 / x.shape[1])
        return (x * lax.rsqrt(ms + EPS) * gain).astype(BF16)

    qf = _dot(latent(pa[:, :MLA_Q_LORA], gql_ref[...]), wq_ref[...])
    kvf = _dot(latent(pa[:, MLA_Q_LORA:MLA_Q_LORA + MLA_KV_LORA], gkvl_ref[...]), wkv_ref[...])
    kr = pa[:, N_A - 2 * LANES:N_A - LANES]
    kr_sw = pa[:, N_A - LANES:]
    cos = cos_ref[0]
    sin = sin_ref[0]
    gq = gq_ref[...]
    gk = gk_ref[...]
    scale = MLA_QK ** -0.5 * LOG2_E
    gq_nope = gq[:, :LANES] * scale
    q_cos = gq[:, LANES:2 * LANES] * cos * scale
    q_sin = gq[:, 2 * LANES:] * sin * scale
    k_rot = kr * (gk[:, LANES:2 * LANES] * cos) + kr_sw * (gk[:, 2 * LANES:] * sin)
    kr_sq = kr * kr
    q_sq = jnp.concatenate(
        [qf[:, MLA_Q_COLS * h:MLA_Q_COLS * h + LANES] ** 2
         + qf[:, MLA_Q_COLS * h + LANES:MLA_Q_COLS * h + 2 * LANES] ** 2 for h in range(HEADS)], axis=1)
    k_sq = jnp.concatenate(
        [kvf[:, 2 * LANES * h:2 * LANES * h + LANES] ** 2 + kr_sq for h in range(HEADS)], axis=1)
    q_rs = lax.rsqrt(_dot(q_sq.astype(BF16), bd_ref[...]) * (1.0 / MLA_QK) + EPS)
    k_rs = lax.rsqrt(_dot(k_sq.astype(BF16), bd_ref[...]) * (1.0 / MLA_QK) + EPS)
    for h in range(HEADS):
        qb = MLA_Q_COLS * h
        rs = q_rs[:, LANES * h:LANES * (h + 1)]
        q_ref[0, h, :, :MLA_NOPE] = (qf[:, qb:qb + LANES] * rs * gq_nope).astype(BF16)
        q_ref[0, h, :, MLA_NOPE:] = ((qf[:, qb + LANES:qb + 2 * LANES] * q_cos
                                      + qf[:, qb + 2 * LANES:qb + 3 * LANES] * q_sin) * rs).astype(BF16)
        kb = 2 * LANES * h
        rs = k_rs[:, LANES * h:LANES * (h + 1)]
        k_ref[0, h, :, :MLA_NOPE] = (kvf[:, kb:kb + LANES] * rs * gk[:, :LANES]).astype(BF16)
        k_ref[0, h, :, MLA_NOPE:] = (k_rot * rs).astype(BF16)
        v_ref[0, h] = kvf[:, kb + LANES:kb + 2 * LANES].astype(BF16)


def _inproj(h, g, w, gql, wq, gkvl, wkv, gq, gk, cos, sin, tm):
    ones = jnp.ones((MLA_Q_LORA, MLA_Q_LORA), BF16)
    head_of = np.arange(HEADS * LANES) // LANES
    bd = jnp.asarray(head_of[:, None] == head_of[None, :], BF16)
    b, s, d = h.shape
    grid = (b, s // tm)
    tok = lambda last: pl.BlockSpec((1, tm, last), lambda i, t: (i, t, 0))
    head = lambda last: pl.BlockSpec((1, HEADS, tm, last), lambda i, t: (i, 0, t, 0))
    return pl.pallas_call(
        _inproj_kernel,
        grid=grid,
        in_specs=[tok(d), _const_spec(g.shape), _const_spec(w.shape), _const_spec(gql.shape),
                  _const_spec(wq.shape), _const_spec(gkvl.shape), _const_spec(wkv.shape),
                  _const_spec(gq.shape), _const_spec(gk.shape), tok(LANES), tok(LANES),
                  _const_spec(ones.shape), _const_spec(bd.shape)],
        out_specs=[head(MLA_QK_PAD), head(MLA_QK_PAD), head(MLA_V), tok(N_B), tok(N_C)],
        out_shape=[jax.ShapeDtypeStruct((b, HEADS, s, MLA_QK_PAD), BF16),
                   jax.ShapeDtypeStruct((b, HEADS, s, MLA_QK_PAD), BF16),
                   jax.ShapeDtypeStruct((b, HEADS, s, MLA_V), BF16),
                   jax.ShapeDtypeStruct((b, s, N_B), F32),
                   jax.ShapeDtypeStruct((b, s, N_C), F32)],
        compiler_params=_params("arbitrary", "arbitrary"),
        name="inproj",
    )(h, g, w, gql, wq, gkvl, wkv, gq, gk, cos, sin, ones, bd)


def _mla_attn_kernel(qa_ref, qb_ref, k_ref, v_ref, o_ref, q_sc, m_sc, l_sc, acc_sc, sa_sc, sb_sc, *, t, nq):
    g = pl.program_id(2)
    q_sc[0:t, :] = qa_ref[0, 0]
    q_sc[t:2 * t, :] = qb_ref[0, 0]
    m_sc[...] = jnp.full(m_sc.shape, MASK_VALUE, F32)
    l_sc[...] = jnp.zeros(l_sc.shape, F32)
    acc_sc[...] = jnp.zeros(acc_sc.shape, F32)

    def scores(rows, kt):
        return _dot_nt(q_sc[rows, :], k_ref[0, 0, pl.ds(pl.multiple_of(kt * t, t), t), :])

    def update(s, rows, kt):
        m_prev = m_sc[rows, :]
        m_new = jnp.maximum(m_prev, jnp.max(s, axis=-1, keepdims=True))
        alpha = jnp.exp2(m_prev - m_new)
        p = jnp.exp2(s - jnp.concatenate([m_new] * (t // LANES), axis=1))
        l_sc[rows, :] = alpha * l_sc[rows, :] + jnp.sum(p, axis=-1, keepdims=True)
        v_tile = v_ref[0, 0, pl.ds(pl.multiple_of(kt * t, t), t), :]
        acc_sc[rows, :] = alpha * acc_sc[rows, :] + _dot(p.astype(BF16), v_tile)
        m_sc[rows, :] = m_new

    qc = lax.broadcasted_iota(jnp.int32, (t, t), 0) // CHUNK
    kc = lax.broadcasted_iota(jnp.int32, (t, t), 1) // CHUNK
    bufs = (sa_sc, sb_sc)
    rows_a, rows_b = pl.ds(0, t), pl.ds(t, t)

    work = []
    for j in range(nq - 1):
        if j < nq // 2 - 1:
            sel = (j >= g).astype(jnp.int32)
            work.append((pl.ds(pl.multiple_of(sel * t, t), t), j - sel * g, False))
        else:
            work.append((rows_b, j - g, False))
    work += [(rows_a, g, True), (rows_b, nq - 1 - g, True)]
    sa_sc[...] = scores(work[0][0], work[0][1])
    for i, (rows, kt, diagonal) in enumerate(work):
        if i + 1 < len(work):
            bufs[(i + 1) % 2][...] = scores(work[i + 1][0], work[i + 1][1])
        s = bufs[i % 2][...]
        if diagonal:
            s = jnp.where(kc <= qc, s, MASK_VALUE)
        update(s, rows, kt)

    o = (acc_sc[...] / l_sc[...]).astype(BF16)
    o_ref[0, 0, 0] = o[:t]
    o_ref[0, 1, 0] = o[t:]


def _mla_attn(q, k, v, t):
    b, nh, s, dq = q.shape
    dv = v.shape[-1]
    nq = s // t
    half = nq // 2
    out = pl.pallas_call(
        functools.partial(_mla_attn_kernel, t=t, nq=nq),
        grid=(b, nh, half),
        in_specs=[pl.BlockSpec((1, 1, t, dq), lambda i, h, j: (i, h, j, 0)),
                  pl.BlockSpec((1, 1, t, dq), lambda i, h, j: (i, h, nq - 1 - j, 0)),
                  pl.BlockSpec((1, 1, s, dq), lambda i, h, j: (i, h, 0, 0)),
                  pl.BlockSpec((1, 1, s, dv), lambda i, h, j: (i, h, 0, 0))],
        out_specs=pl.BlockSpec((1, 2, 1, t, dv), lambda i, h, j: (i, 0, j, 0, h)),
        out_shape=jax.ShapeDtypeStruct((b, 2, half, t, nh * dv), BF16),
        scratch_shapes=[pltpu.VMEM((2 * t, dq), BF16), pltpu.VMEM((2 * t, LANES), F32),
                        pltpu.VMEM((2 * t, LANES), F32), pltpu.VMEM((2 * t, dv), F32),
                        pltpu.VMEM((t, t), F32), pltpu.VMEM((t, t), F32)],
        compiler_params=_params("arbitrary", "arbitrary", "arbitrary"),
        name="mla_attn",
    )(q, q, k, v)
    return jnp.concatenate([out[:, 0], out[:, 1, ::-1]], axis=1).reshape(b, s, nh * dv)


def _rec_consts(dk, dv):
    hk, hv = HEADS * dk, HEADS * dv
    c, blk = REC_CHUNK, REC_BLOCK
    head_k = np.arange(hk) // dk
    head_v = np.arange(hv) // dv
    head_r = np.repeat(np.arange(HEADS), blk)
    ltri = (np.arange(c)[:, None] >= np.arange(c)[None, :])
    ind = head_k[:, None] == head_v[None, :]
    col = (head_k[None, :] * blk + np.arange(blk)[:, None]).reshape(-1)
    ind_diag = col[:, None] == np.arange(HEADS * blk)[None, :]
    indv = (head_v[:, None] == head_v[None, :]) / dv
    hm_k = head_r[:, None] == head_k[None, :]
    hm_v = head_r[:, None] == head_v[None, :]
    return (jnp.asarray(ltri, BF16), jnp.asarray(ind_diag, BF16), jnp.asarray(indv, BF16),
            jnp.asarray(hm_k, F32), jnp.asarray(hm_v, F32), jnp.asarray(ind.T, F32))


def _rec_tile(q, k, v, g, st_ref, plo_sc, phi_sc, o_sc, ltri_ref, ind_ref, hmk_ref, hmv_ref, hms_ref):
    ts, hk = q.shape
    hv = v.shape[1]
    c, blk = REC_CHUNK, REC_BLOCK
    nb = c // blk
    half = blk // 2
    chunks = [slice(i * c, (i + 1) * c) for i in range(ts // c)]
    g2 = g * LOG2_E
    g_hi = g2.astype(BF16)
    g_lo = (g2 - g_hi.astype(F32)).astype(BF16)
    ltri = ltri_ref[...]
    b = [_dot(ltri, g_hi[r]) + _dot(ltri, g_lo[r]) for r in chunks]

    sub = lax.broadcasted_iota(jnp.int32, (nb, half, hk), 1)
    diag = []
    for ci, r in enumerate(chunks):
        b4 = b[ci].reshape(nb, 2, half, hk)
        q4 = q[r].reshape(nb, 2, half, hk)
        k4 = k[r].reshape(nb, 2, half, hk)
        for jj in range(blk):
            hf, jr = divmod(jj, half)
            bj = b4[:, hf, jr:jr + 1, :]
            kj = k4[:, hf, jr:jr + 1, :]
            p = jnp.where(sub >= jr, q4[:, hf] * kj * jnp.exp2(b4[:, hf] - bj), 0.0)
            if hf == 0:
                plo_sc[ci, :, jj * hk:(jj + 1) * hk] = p.reshape(c // 2, hk).astype(BF16)
                p = q4[:, 1] * kj * jnp.exp2(b4[:, 1] - bj)
            phi_sc[ci, :, jj * hk:(jj + 1) * hk] = p.reshape(c // 2, hk).astype(BF16)
        d_lo = _dot(plo_sc[ci], ind_ref[:half * hk, :]).reshape(nb, half, HEADS * blk)
        d_hi = _dot(phi_sc[ci], ind_ref[...]).reshape(nb, half, HEADS * blk)
        diag.append(jnp.stack([d_lo, d_hi], axis=1).reshape(c, HEADS * blk))

    hmk = hmk_ref[...]
    hmv = hmv_ref[...]
    for jb in range(nb):
        lo, hi = blk * jb, blk * (jb + 1)
        for ci, r in enumerate(chunks):
            qc, kc, vc, bc = q[r], k[r], v[r], b[ci]
            a = diag[ci][lo:hi, :]
            if hi < c:
                b_end = bc[hi - 1:hi, :]
                qj = (qc[hi:, :] * jnp.exp2(bc[hi:, :] - b_end)).astype(BF16)
                kj = kc[lo:hi, :] * jnp.exp2(b_end - bc[lo:hi, :])
                kbd = (jnp.concatenate([kj] * HEADS, axis=0) * hmk).astype(BF16)
                a = jnp.concatenate([a, _dot_nt(qj, kbd)], axis=0)
            vbd = (jnp.concatenate([vc[lo:hi, :]] * HEADS, axis=0) * hmv).astype(BF16)
            contrib = _dot(a.astype(BF16), vbd)
            if jb == 0:
                o_sc[r, :] = contrib
            else:
                o_sc[r.start + lo:r.stop, :] += contrib

    upd = []
    for ci, r in enumerate(chunks):
        kh = (k[r] * jnp.exp2(b[ci][c - 1:c, :] - b[ci])).astype(BF16)
        upd.append(lax.dot_general(v[r].astype(BF16), kh, TN_DIMS, preferred_element_type=F32))
    st = st_ref[...]
    for ci, r in enumerate(chunks):
        o_sc[r, :] += _dot_nt((q[r] * jnp.exp2(b[ci])).astype(BF16), st.astype(BF16))
        st = st * jnp.exp2(b[ci][c - 1:c, :]) + upd[ci] * hms_ref[...]
    st_ref[...] = st


def _head_rms_gate(o, gate, gain, indv_ref):
    ms = _dot((o * o).astype(BF16), indv_ref[...])
    return o * lax.rsqrt(ms + EPS) * gain * (gate * _sigmoid(gate))


def _gla_kernel(pb_ref, wg_ref, bg_ref, gn_ref, ltri_ref, ind_ref, indv_ref, hmk_ref, hmv_ref, hms_ref,
                y_ref, st_ref, plo_sc, phi_sc, o_sc):
    @pl.when(pl.program_id(1) == 0)
    def _():
        st_ref[...] = jnp.zeros(st_ref.shape, F32)

    hk, hv = HEADS * GLA_DK, HEADS * GLA_DV
    x = pb_ref[0]
    q = x[:, :hk] * (GLA_DK ** -0.5)
    k = x[:, hk:2 * hk]
    v = x[:, 2 * hk:2 * hk + hv]
    gate = x[:, 2 * hk + hv:2 * hk + 2 * hv]
    logits = _dot(x[:, 2 * hk + 2 * hv:].astype(BF16), wg_ref[...]) + bg_ref[...]
    g = _log_sigmoid(logits) * (1.0 / GLA_TAU)
    _rec_tile(q, k, v, g, st_ref, plo_sc, phi_sc, o_sc, ltri_ref, ind_ref, hmk_ref, hmv_ref, hms_ref)
    y_ref[0] = _head_rms_gate(o_sc[...], gate, gn_ref[...], indv_ref).astype(BF16)


def _hgrn_kernel(pc_ref, lb_ref, gn_ref, ltri_ref, ind_ref, indv_ref, hmk_ref, hmv_ref, hms_ref,
                 y_ref, st_ref, plo_sc, phi_sc, o_sc):
    @pl.when(pl.program_id(1) == 0)
    def _():
        st_ref[...] = jnp.zeros(st_ref.shape, F32)

    hk, hv = HEADS * HGRN_DK, HEADS * HGRN_DV
    x = pc_ref[0]
    cq = x[:, :hk]
    z = x[:, hk:2 * hk]
    v = x[:, 2 * hk:2 * hk + hv]
    gate = x[:, 2 * hk + hv:]
    lb = lb_ref[...]
    f = jnp.maximum(lb, LB_FLOOR) + (1.0 - lb) * _sigmoid(z)
    g = jnp.minimum(jnp.log(f), 0.0)
    k = (1.0 - lb) * _sigmoid(-z)
    q = cq * _sigmoid(cq) * (HGRN_DK ** -0.5)
    _rec_tile(q, k, v, g, st_ref, plo_sc, phi_sc, o_sc, ltri_ref, ind_ref, hmk_ref, hmv_ref, hms_ref)
    y_ref[0] = _head_rms_gate(o_sc[...], gate, gn_ref[...], indv_ref).astype(BF16)


def _recurrence(kernel, name, x, small, consts, dk, dv, ts):
    b, s, width = x.shape
    hk, hv = HEADS * dk, HEADS * dv
    ltri, ind, indv, hmk, hmv, hms = consts
    operands = (x, *small, ltri, ind, indv, hmk, hmv, hms)
    in_specs = [pl.BlockSpec((1, ts, width), lambda i, t: (i, t, 0))]
    in_specs += [_const_spec(a.shape) for a in operands[1:]]
    return pl.pallas_call(
        kernel,
        grid=(b, s // ts),
        in_specs=in_specs,
        out_specs=pl.BlockSpec((1, ts, hv), lambda i, t: (i, t, 0)),
        out_shape=jax.ShapeDtypeStruct((b, s, hv), BF16),
        scratch_shapes=[pltpu.VMEM((hv, hk), F32),
                        pltpu.VMEM((ts // REC_CHUNK, REC_CHUNK // 2, REC_BLOCK // 2 * hk), BF16),
                        pltpu.VMEM((ts // REC_CHUNK, REC_CHUNK // 2, REC_BLOCK * hk), BF16),
                        pltpu.VMEM((ts, hv), F32)],
        compiler_params=_params("arbitrary", "arbitrary"),
        name=name,
    )(*operands)


def _mixca_kernel(h_ref, ya_ref, yb_ref, yc_ref, wa_ref, wb_ref, wc_ref, g_ref, wq_ref, gq_ref,
                  km_ref, vm_ref, wo_ref, out_ref):
    h1 = (h_ref[0] + _dot(ya_ref[0], wa_ref[...]) + _dot(yb_ref[0], wb_ref[...])
          + _dot(yc_ref[0], wc_ref[...]))
    qc = _dot(_rms(h1, g_ref[...]).astype(BF16), wq_ref[...])
    heads = []
    for h in range(HEADS):
        qn = _rms(qc[:, CA_DIM * h:CA_DIM * (h + 1)], gq_ref[...]) * (CA_DIM ** -0.5)
        s = _dot_nt(qn.astype(BF16), km_ref[0, h])
        p = jnp.exp(s - jnp.max(s, axis=-1, keepdims=True))
        o = _dot(p.astype(BF16), vm_ref[0, h]) / jnp.sum(p, axis=-1, keepdims=True)
        heads.append(o.astype(BF16))
    out_ref[0] = h1 + _dot(jnp.concatenate(heads, axis=-1), wo_ref[...])


def _mixca(h, ya, yb, yc, wa, wb, wc, g, wq, gq, km, vm, wo, tm):
    b, s, d = h.shape
    tok = lambda last: pl.BlockSpec((1, tm, last), lambda i, t: (i, t, 0))
    mem = pl.BlockSpec((1,) + km.shape[1:], lambda i, t: (i, 0, 0, 0))
    return pl.pallas_call(
        _mixca_kernel,
        grid=(b, s // tm),
        in_specs=[tok(d), tok(ya.shape[-1]), tok(yb.shape[-1]), tok(yc.shape[-1]),
                  _const_spec(wa.shape), _const_spec(wb.shape), _const_spec(wc.shape),
                  _const_spec(g.shape), _const_spec(wq.shape), _const_spec(gq.shape),
                  mem, mem, _const_spec(wo.shape)],
        out_specs=tok(d),
        out_shape=jax.ShapeDtypeStruct((b, s, d), F32),
        compiler_params=_params("arbitrary", "arbitrary"),
        name="mix_ca",
    )(h, ya, yb, yc, wa, wb, wc, g, wq, gq, km, vm, wo)


def _ffn_kernel(h_ref, g_ref, wg_ref, wu_ref, cw_ref, cb_ref, wo_ref, out_ref, gs_sc, carry_sc, *, bounds):
    tm = h_ref.shape[1]

    @pl.when(pl.program_id(1) == 0)
    def _():
        carry_sc[...] = jnp.zeros(carry_sc.shape, F32)

    x = h_ref[0]
    n = _rms(x, g_ref[...]).astype(BF16)
    acc = x
    for c0, c1 in zip(bounds[:-1], bounds[1:]):
        fc = c1 - c0
        gate = _dot(n, wg_ref[:, c0:c1])
        gs_sc[0:8, :fc] = carry_sc[:, c0:c1]
        gs_sc[8:8 + tm, :fc] = gate
        carry_sc[:, c0:c1] = gate[tm - 8:, :]
        conv = (cb_ref[:, c0:c1] + gs_sc[6:6 + tm, :fc] * cw_ref[0:1, c0:c1]
                + gs_sc[7:7 + tm, :fc] * cw_ref[1:2, c0:c1] + gate * cw_ref[2:3, c0:c1])
        up = _dot(n, wu_ref[:, c0:c1])
        act = (conv * _sigmoid(conv) * up).astype(BF16)
        acc = acc + _dot(act, wo_ref[c0:c1, :])
    out_ref[0] = acc


def _ffn(h, g, wg, wu, cw, cb, wo, tm):
    b, s, d = h.shape
    d_ff = wg.shape[1]
    n_tiles = -(-d_ff // MXU_COLS)
    bounds = (0, min(d_ff, (n_tiles + 1) // 2 * MXU_COLS), d_ff)
    fc = max(bounds[1], d_ff - bounds[1])
    tok = pl.BlockSpec((1, tm, d), lambda i, t: (i, t, 0))
    return pl.pallas_call(
        functools.partial(_ffn_kernel, bounds=bounds),
        grid=(b, s // tm),
        in_specs=[tok, _const_spec(g.shape), _const_spec(wg.shape), _const_spec(wu.shape),
                  _const_spec(cw.shape), _const_spec(cb.shape), _const_spec(wo.shape)],
        out_specs=tok,
        out_shape=jax.ShapeDtypeStruct((b, s, d), F32),
        scratch_shapes=[pltpu.VMEM((tm + 8, fc), F32), pltpu.VMEM((8, d_ff), F32)],
        compiler_params=_params("arbitrary", "arbitrary"),
        name="ffn",
    )(h, g, wg, wu, cw, cb, wo)


def _rope_lanes(x):
    half = MLA_ROPE // 2
    z = jnp.zeros(x.shape[:-1] + (LANES // 2 - half,), x.dtype)
    return jnp.concatenate([x[..., :half], z, x[..., half:], z], axis=-1)


def _rope_lanes_swapped(x):
    half = MLA_ROPE // 2
    return _rope_lanes(jnp.concatenate([x[..., half:], x[..., :half]], axis=-1))


def _qk_lanes(x):
    rope = x[..., MLA_NOPE:]
    return jnp.concatenate([x[..., :MLA_NOPE], _rope_lanes(rope), _rope_lanes_swapped(rope)], axis=-1)


def _relayout_w_in(w):
    offs = np.cumsum([0, MLA_Q_LORA, MLA_KV_LORA, MLA_ROPE, HEADS * GLA_DK, HEADS * GLA_DK, HEADS * GLA_DV,
                      GLA_RANK, HEADS * GLA_DV, HEADS * HGRN_DK, HEADS * HGRN_DK, HEADS * HGRN_DV,
                      HEADS * HGRN_DV])
    seg = lambda i: w[:, offs[i]:offs[i + 1]]
    gate_lr = jnp.pad(seg(6), ((0, 0), (0, LANES - GLA_RANK)))
    return jnp.concatenate([seg(0), seg(1), _rope_lanes(seg(2)), _rope_lanes_swapped(seg(2)),
                            seg(3), seg(4), seg(5), seg(7), gate_lr,
                            seg(8), seg(9), seg(10), seg(11)], axis=-1)


def _pick(n, pref):
    return pref if n % pref == 0 else n


def kernel(x, mem, positions, norm_mix, w_in, mla_q_lat_norm, w_q_up, mla_kv_lat_norm, w_kv_up, mla_q_norm,
           mla_k_norm, gla_w_gate, gla_b_gate, gla_out_norm, hgrn_lb_logits, hgrn_out_norm, w_mix_out,
           norm_ca, norm_mem, w_ca_q, w_ca_kv, ca_q_norm, ca_k_norm, w_ca_out, norm_ffn, w_ffn_in,
           ffn_conv_w, ffn_conv_b, w_ffn_out):
    b, s, d = x.shape
    depth = w_in.shape[0]
    d_ff = w_ffn_out.shape[1]
    tm = _pick(s, 512)
    t_attn = _pick(s, 512)
    ts_rec = _pick(s, 512)

    gamma = jax.nn.softmax(hgrn_lb_logits.astype(F32), axis=0)
    lower_bounds = jnp.cumsum(gamma, axis=0) - gamma[0]
    inv_freq = ROPE_THETA ** (-jnp.arange(0, MLA_ROPE, 2, dtype=F32) / MLA_ROPE)
    ang = positions.astype(F32)[..., None] * inv_freq
    cos = _rope_lanes(jnp.concatenate([jnp.cos(ang), jnp.cos(ang)], axis=-1))
    sin = _rope_lanes(jnp.concatenate([-jnp.sin(ang), jnp.sin(ang)], axis=-1))

    km, vm = _memkv(mem, norm_mem, w_ca_kv.astype(BF16), ca_k_norm)
    gla_consts = _rec_consts(GLA_DK, GLA_DV)
    hgrn_consts = _rec_consts(HGRN_DK, HGRN_DV)
    row = lambda a: a.reshape(1, -1)
    n_a = HEADS * MLA_V
    n_b = HEADS * GLA_DV

    h = x
    for l in range(depth):
        w_l = _relayout_w_in(w_in[l]).astype(BF16)
        wq_l = _qk_lanes(w_q_up[l].reshape(MLA_Q_LORA, HEADS, MLA_QK)).reshape(MLA_Q_LORA, -1).astype(BF16)
        q, k, v, pb, pc = _inproj(
            h, row(norm_mix[l]), w_l, row(mla_q_lat_norm[l]), wq_l, row(mla_kv_lat_norm[l]),
            w_kv_up[l].astype(BF16), row(_qk_lanes(mla_q_norm[l])), row(_qk_lanes(mla_k_norm[l])),
            cos, sin, tm)
        ya = _mla_attn(q, k, v, t_attn)
        wg_l = jnp.pad(gla_w_gate[l], ((0, LANES - GLA_RANK), (0, 0))).astype(BF16)
        yb = _recurrence(_gla_kernel, "gla", pb,
                         (wg_l, row(gla_b_gate[l]), row(jnp.tile(gla_out_norm[l], HEADS))),
                         gla_consts, GLA_DK, GLA_DV, ts_rec)
        yc = _recurrence(_hgrn_kernel, "hgrn", pc,
                         (row(lower_bounds[l]), row(jnp.tile(hgrn_out_norm[l], HEADS))),
                         hgrn_consts, HGRN_DK, HGRN_DV, ts_rec)
        wm = w_mix_out[l].astype(BF16)
        h = _mixca(h, ya, yb, yc, wm[:n_a], wm[n_a:n_a + n_b], wm[n_a + n_b:], row(norm_ca[l]),
                   w_ca_q[l].astype(BF16), row(ca_q_norm[l]), km[l], vm[l], w_ca_out[l].astype(BF16), tm)
        wf = w_ffn_in[l].astype(BF16)
        cw = jnp.pad(ffn_conv_w[l], ((0, 8 - ffn_conv_w.shape[1]), (0, 0)))
        h = _ffn(h, row(norm_ffn[l]), wf[:, :d_ff], wf[:, d_ff:], cw, row(ffn_conv_b[l]),
                 w_ffn_out[l].astype(BF16), tm)
    return h
```

```python
import functools

import numpy as np
import jax
import jax.numpy as jnp
from jax import lax
from jax.experimental import pallas as pl
from jax.experimental.pallas import tpu as pltpu

F32 = jnp.float32
BF16 = jnp.bfloat16

EPS = 1e-6
MASK_VALUE = -1e30
LB_FLOOR = 1e-30
CHUNK = 64
ROPE_THETA = 10000.0
LOG2_E = 1.4426950408889634

HEADS = 4
MLA_Q_LORA = 256
MLA_KV_LORA = 256
MLA_NOPE = 128
MLA_ROPE = 64
MLA_V = 128
MLA_QK = MLA_NOPE + MLA_ROPE
MLA_QK_PAD = 256
GLA_DK = 32
GLA_DV = 64
GLA_RANK = 16
GLA_TAU = 16.0
HGRN_DK = 64
HGRN_DV = 64
CA_DIM = 128
MLA_Q_COLS = 384
N_A = 768
N_B = 896
N_C = 1024

LANES = 128
MXU_COLS = 256
REC_CHUNK = 128
REC_BLOCK = 16
VMEM_LIMIT = 56 * 1024 * 1024

NT_DIMS = (((1,), (1,)), ((), ()))
TN_DIMS = (((0,), (0,)), ((), ()))


def _dot(a, b):
    return jnp.dot(a, b, preferred_element_type=F32)


def _dot_nt(a, b):
    return lax.dot_general(a, b, NT_DIMS, preferred_element_type=F32)


def _rms(x, gain, dim=None):
    dim = x.shape[-1] if dim is None else dim
    ms = jnp.sum(x * x, axis=-1, keepdims=True) * (1.0 / dim)
    return x * lax.rsqrt(ms + EPS) * gain


def _sigmoid(x):
    return 1.0 / (1.0 + jnp.exp(-x))


def _log_sigmoid(x):
    return jnp.minimum(x, 0.0) - jnp.log1p(jnp.exp(-jnp.abs(x)))


def _params(*sem):
    return pltpu.CompilerParams(dimension_semantics=sem, vmem_limit_bytes=VMEM_LIMIT)


def _const_spec(shape):
    nd = len(shape)
    return pl.BlockSpec(shape, lambda *_: (0,) * nd)


def _memkv_kernel(mem_ref, g_ref, w_ref, gk_ref, k_ref, v_ref):
    n = _rms(mem_ref[0], g_ref[0]).astype(BF16)
    kv = _dot(n, w_ref[0])
    for h in range(HEADS):
        base = 2 * CA_DIM * h
        k_ref[0, 0, h] = _rms(kv[:, base:base + CA_DIM], gk_ref[0]).astype(BF16)
        v_ref[0, 0, h] = kv[:, base + CA_DIM:base + 2 * CA_DIM].astype(BF16)


def _memkv(mem, norm_mem, w_ca_kv, ca_k_norm):
    depth, d = norm_mem.shape
    b, m, _ = mem.shape
    out = jax.ShapeDtypeStruct((depth, b, HEADS, m, CA_DIM), BF16)
    spec_o = pl.BlockSpec((1, 1, HEADS, m, CA_DIM), lambda l, i: (l, i, 0, 0, 0))
    return pl.pallas_call(
        _memkv_kernel,
        grid=(depth, b),
        in_specs=[
            pl.BlockSpec((1, m, d), lambda l, i: (i, 0, 0)),
            pl.BlockSpec((1, 1, d), lambda l, i: (l, 0, 0)),
            pl.BlockSpec((1, d, w_ca_kv.shape[-1]), lambda l, i: (l, 0, 0)),
            pl.BlockSpec((1, 1, CA_DIM), lambda l, i: (l, 0, 0)),
        ],
        out_specs=[spec_o, spec_o],
        out_shape=[out, out],
        compiler_params=_params("arbitrary", "arbitrary"),
        name="memkv",
    )(mem, norm_mem.reshape(depth, 1, d), w_ca_kv, ca_k_norm.reshape(depth, 1, CA_DIM))


def _inproj_kernel(h_ref, g_ref, w_ref, gql_ref, wq_ref, gkvl_ref, wkv_ref, gq_ref, gk_ref,
                   cos_ref, sin_ref, ones_ref, bd_ref, q_ref, k_ref, v_ref, pb_ref, pc_ref):
    n = _rms(h_ref[0], g_ref[...]).astype(BF16)
    pa = _dot(n, w_ref[:, :N_A])
    pb_ref[0] = _dot(n, w_ref[:, N_A:N_A + N_B])
    pc_ref[0] = _dot(n, w_ref[:, N_A + N_B:])

    def latent(x, gain):
        ms = _dot((x * x).astype(BF16), ones_ref[...]) * (1.0 / x.shape[1])
        return (x * lax.rsqrt(ms + EPS) * gain).astype(BF16)

    qf = _dot(latent(pa[:, :MLA_Q_LORA], gql_ref[...]), wq_ref[...])
    kvf = _dot(latent(pa[:, MLA_Q_LORA:MLA_Q_LORA + MLA_KV_LORA], gkvl_ref[...]), wkv_ref[...])
    kr = pa[:, N_A - 2 * LANES:N_A - LANES]
    kr_sw = pa[:, N_A - LANES:]
    cos = cos_ref[0]
    sin = sin_ref[0]
    gq = gq_ref[...]
    gk = gk_ref[...]
    scale = MLA_QK ** -0.5 * LOG2_E
    gq_nope = gq[:, :LANES] * scale
    q_cos = gq[:, LANES:2 * LANES] * cos * scale
    q_sin = gq[:, 2 * LANES:] * sin * scale
    k_rot = kr * (gk[:, LANES:2 * LANES] * cos) + kr_sw * (gk[:, 2 * LANES:] * sin)
    kr_sq = kr * kr
    q_sq = jnp.concatenate(
        [qf[:, MLA_Q_COLS * h:MLA_Q_COLS * h + LANES] ** 2
         + qf[:, MLA_Q_COLS * h + LANES:MLA_Q_COLS * h + 2 * LANES] ** 2 for h in range(HEADS)], axis=1)
    k_sq = jnp.concatenate(
        [kvf[:, 2 * LANES * h:2 * LANES * h + LANES] ** 2 + kr_sq for h in range(HEADS)], axis=1)
    q_rs = lax.rsqrt(_dot(q_sq.astype(BF16), bd_ref[...]) * (1.0 / MLA_QK) + EPS)
    k_rs = lax.rsqrt(_dot(k_sq.astype(BF16), bd_ref[...]) * (1.0 / MLA_QK) + EPS)
    for h in range(HEADS):
        qb = MLA_Q_COLS * h
        rs = q_rs[:, LANES * h:LANES * (h + 1)]
        q_ref[0, h, :, :MLA_NOPE] = (qf[:, qb:qb + LANES] * rs * gq_nope).astype(BF16)
        q_ref[0, h, :, MLA_NOPE:] = ((qf[:, qb + LANES:qb + 2 * LANES] * q_cos
                                      + qf[:, qb + 2 * LANES:qb + 3 * LANES] * q_sin) * rs).astype(BF16)
        kb = 2 * LANES * h
        rs = k_rs[:, LANES * h:LANES * (h + 1)]
        k_ref[0, h, :, :MLA_NOPE] = (kvf[:, kb:kb + LANES] * rs * gk[:, :LANES]).astype(BF16)
        k_ref[0, h, :, MLA_NOPE:] = (k_rot * rs).astype(BF16)
        v_ref[0, h] = kvf[:, kb + LANES:kb + 2 * LANES].astype(BF16)


def _inproj(h, g, w, gql, wq, gkvl, wkv, gq, gk, cos, sin, tm):
    ones = jnp.ones((MLA_Q_LORA, MLA_Q_LORA), BF16)
    head_of = np.arange(HEADS * LANES) // LANES
    bd = jnp.asarray(head_of[:, None] == head_of[None, :], BF16)
    b, s, d = h.shape
    grid = (b, s // tm)
    tok = lambda last: pl.BlockSpec((1, tm, last), lambda i, t: (i, t, 0))
    head = lambda last: pl.BlockSpec((1, HEADS, tm, last), lambda i, t: (i, 0, t, 0))
    return pl.pallas_call(
        _inproj_kernel,
        grid=grid,
        in_specs=[tok(d), _const_spec(g.shape), _const_spec(w.shape), _const_spec(gql.shape),
                  _const_spec(wq.shape), _const_spec(gkvl.shape), _const_spec(wkv.shape),
                  _const_spec(gq.shape), _const_spec(gk.shape), tok(LANES), tok(LANES),
                  _const_spec(ones.shape), _const_spec(bd.shape)],
        out_specs=[head(MLA_QK_PAD), head(MLA_QK_PAD), head(MLA_V), tok(N_B), tok(N_C)],
        out_shape=[jax.ShapeDtypeStruct((b, HEADS, s, MLA_QK_PAD), BF16),
                   jax.ShapeDtypeStruct((b, HEADS, s, MLA_QK_PAD), BF16),
                   jax.ShapeDtypeStruct((b, HEADS, s, MLA_V), BF16),
                   jax.ShapeDtypeStruct((b, s, N_B), F32),
                   jax.ShapeDtypeStruct((b, s, N_C), F32)],
        compiler_params=_params("arbitrary", "arbitrary"),
        name="inproj",
    )(h, g, w, gql, wq, gkvl, wkv, gq, gk, cos, sin, ones, bd)


def _mla_attn_kernel(qa_ref, qb_ref, k_ref, v_ref, o_ref, q_sc, m_sc, l_sc, acc_sc, sa_sc, sb_sc, *, t, nq):
    g = pl.program_id(2)
    q_sc[0:t, :] = qa_ref[0, 0]
    q_sc[t:2 * t, :] = qb_ref[0, 0]
    m_sc[...] = jnp.full(m_sc.shape, MASK_VALUE, F32)
    l_sc[...] = jnp.zeros(l_sc.shape, F32)
    acc_sc[...] = jnp.zeros(acc_sc.shape, F32)

    def scores(rows, kt):
        return _dot_nt(q_sc[rows, :], k_ref[0, 0, pl.ds(pl.multiple_of(kt * t, t), t), :])

    def update(s, rows, kt):
        m_prev = m_sc[rows, :]
        m_new = jnp.maximum(m_prev, jnp.max(s, axis=-1, keepdims=True))
        alpha = jnp.exp2(m_prev - m_new)
        p = jnp.exp2(s - jnp.concatenate([m_new] * (t // LANES), axis=1))
        l_sc[rows, :] = alpha * l_sc[rows, :] + jnp.sum(p, axis=-1, keepdims=True)
        v_tile = v_ref[0, 0, pl.ds(pl.multiple_of(kt * t, t), t), :]
        acc_sc[rows, :] = alpha * acc_sc[rows, :] + _dot(p.astype(BF16), v_tile)
        m_sc[rows, :] = m_new

    qc = lax.broadcasted_iota(jnp.int32, (t, t), 0) // CHUNK
    kc = lax.broadcasted_iota(jnp.int32, (t, t), 1) // CHUNK
    bufs = (sa_sc, sb_sc)
    rows_a, rows_b = pl.ds(0, t), pl.ds(t, t)

    work = []
    for j in range(nq - 1):
        if j < nq // 2 - 1:
            sel = (j >= g).astype(jnp.int32)
            work.append((pl.ds(pl.multiple_of(sel * t, t), t), j - sel * g, False))
        else:
            work.append((rows_b, j - g, False))
    work += [(rows_a, g, True), (rows_b, nq - 1 - g, True)]
    sa_sc[...] = scores(work[0][0], work[0][1])
    for i, (rows, kt, diagonal) in enumerate(work):
        if i + 1 < len(work):
            bufs[(i + 1) % 2][...] = scores(work[i + 1][0], work[i + 1][1])
        s = bufs[i % 2][...]
        if diagonal:
            s = jnp.where(kc <= qc, s, MASK_VALUE)
        update(s, rows, kt)

    o = (acc_sc[...] / l_sc[...]).astype(BF16)
    o_ref[0, 0, 0] = o[:t]
    o_ref[0, 1, 0] = o[t:]


def _mla_attn(q, k, v, t):
    b, nh, s, dq = q.shape
    dv = v.shape[-1]
    nq = s // t
    half = nq // 2
    return pl.pallas_call(
        functools.partial(_mla_attn_kernel, t=t, nq=nq),
        grid=(b, nh, half),
        in_specs=[pl.BlockSpec((1, 1, t, dq), lambda i, h, j: (i, h, j, 0)),
                  pl.BlockSpec((1, 1, t, dq), lambda i, h, j: (i, h, nq - 1 - j, 0)),
                  pl.BlockSpec((1, 1, s, dq), lambda i, h, j: (i, h, 0, 0)),
                  pl.BlockSpec((1, 1, s, dv), lambda i, h, j: (i, h, 0, 0))],
        out_specs=pl.BlockSpec((1, 2, 1, t, dv), lambda i, h, j: (i, 0, j, 0, h)),
        out_shape=jax.ShapeDtypeStruct((b, 2, half, t, nh * dv), BF16),
        scratch_shapes=[pltpu.VMEM((2 * t, dq), BF16), pltpu.VMEM((2 * t, LANES), F32),
                        pltpu.VMEM((2 * t, LANES), F32), pltpu.VMEM((2 * t, dv), F32),
                        pltpu.VMEM((t, t), F32), pltpu.VMEM((t, t), F32)],
        compiler_params=_params("arbitrary", "arbitrary", "arbitrary"),
        name="mla_attn",
    )(q, q, k, v)


def _rec_consts(dk, dv):
    hk, hv = HEADS * dk, HEADS * dv
    c, blk = REC_CHUNK, REC_BLOCK
    head_k = np.arange(hk) // dk
    head_v = np.arange(hv) // dv
    head_r = np.repeat(np.arange(HEADS), blk)
    ltri = (np.arange(c)[:, None] >= np.arange(c)[None, :])
    ind = head_k[:, None] == head_v[None, :]
    col = (head_k[None, :] * blk + np.arange(blk)[:, None]).reshape(-1)
    ind_diag = col[:, None] == np.arange(HEADS * blk)[None, :]
    indv = (head_v[:, None] == head_v[None, :]) / dv
    hm_k = head_r[:, None] == head_k[None, :]
    hm_v = head_r[:, None] == head_v[None, :]
    return (jnp.asarray(ltri, BF16), jnp.asarray(ind_diag, BF16), jnp.asarray(indv, BF16),
            jnp.asarray(hm_k, F32), jnp.asarray(hm_v, F32), jnp.asarray(ind.T, F32))


def _rec_tile(q, k, v, g, st_ref, plo_sc, phi_sc, o_sc, ltri_ref, ind_ref, hmk_ref, hmv_ref, hms_ref):
    ts, hk = q.shape
    hv = v.shape[1]
    c, blk = REC_CHUNK, REC_BLOCK
    nb = c // blk
    half = blk // 2
    chunks = [slice(i * c, (i + 1) * c) for i in range(ts // c)]
    g2 = g * LOG2_E
    g_hi = g2.astype(BF16)
    g_lo = (g2 - g_hi.astype(F32)).astype(BF16)
    ltri = ltri_ref[...]
    b = [_dot(ltri, g_hi[r]) + _dot(ltri, g_lo[r]) for r in chunks]

    sub = lax.broadcasted_iota(jnp.int32, (nb, half, hk), 1)
    diag = []
    for ci, r in enumerate(chunks):
        b4 = b[ci].reshape(nb, 2, half, hk)
        q4 = q[r].reshape(nb, 2, half, hk)
        k4 = k[r].reshape(nb, 2, half, hk)
        for jj in range(blk):
            hf, jr = divmod(jj, half)
            bj = b4[:, hf, jr:jr + 1, :]
            kj = k4[:, hf, jr:jr + 1, :]
            p = jnp.where(sub >= jr, q4[:, hf] * kj * jnp.exp2(b4[:, hf] - bj), 0.0)
            if hf == 0:
                plo_sc[ci, :, jj * hk:(jj + 1) * hk] = p.reshape(c // 2, hk).astype(BF16)
                p = q4[:, 1] * kj * jnp.exp2(b4[:, 1] - bj)
            phi_sc[ci, :, jj * hk:(jj + 1) * hk] = p.reshape(c // 2, hk).astype(BF16)
        d_lo = _dot(plo_sc[ci], ind_ref[:half * hk, :]).reshape(nb, half, HEADS * blk)
        d_hi = _dot(phi_sc[ci], ind_ref[...]).reshape(nb, half, HEADS * blk)
        diag.append(jnp.stack([d_lo, d_hi], axis=1).reshape(c, HEADS * blk))

    hmk = hmk_ref[...]
    hmv = hmv_ref[...]
    for jb in range(nb):
        lo, hi = blk * jb, blk * (jb + 1)
        for ci, r in enumerate(chunks):
            qc, kc, vc, bc = q[r], k[r], v[r], b[ci]
            a = diag[ci][lo:hi, :]
            if hi < c:
                b_end = bc[hi - 1:hi, :]
                qj = (qc[hi:, :] * jnp.exp2(bc[hi:, :] - b_end)).astype(BF16)
                kj = kc[lo:hi, :] * jnp.exp2(b_end - bc[lo:hi, :])
                kbd = (jnp.concatenate([kj] * HEADS, axis=0) * hmk).astype(BF16)
                a = jnp.concatenate([a, _dot_nt(qj, kbd)], axis=0)
            vbd = (jnp.concatenate([vc[lo:hi, :]] * HEADS, axis=0) * hmv).astype(BF16)
            contrib = _dot(a.astype(BF16), vbd)
            if jb == 0:
                o_sc[r, :] = contrib
            else:
                o_sc[r.start + lo:r.stop, :] += contrib

    upd = []
    for ci, r in enumerate(chunks):
        kh = (k[r] * jnp.exp2(b[ci][c - 1:c, :] - b[ci])).astype(BF16)
        upd.append(lax.dot_general(v[r].astype(BF16), kh, TN_DIMS, preferred_element_type=F32))
    st = st_ref[...]
    for ci, r in enumerate(chunks):
        o_sc[r, :] += _dot_nt((q[r] * jnp.exp2(b[ci])).astype(BF16), st.astype(BF16))
        st = st * jnp.exp2(b[ci][c - 1:c, :]) + upd[ci] * hms_ref[...]
    st_ref[...] = st


def _head_rms_gate(o, gate, gain, indv_ref):
    ms = _dot((o * o).astype(BF16), indv_ref[...])
    return o * lax.rsqrt(ms + EPS) * gain * (gate * _sigmoid(gate))


def _gla_kernel(pb_ref, wg_ref, bg_ref, gn_ref, ltri_ref, ind_ref, indv_ref, hmk_ref, hmv_ref, hms_ref,
                y_ref, st_ref, plo_sc, phi_sc, o_sc):
    @pl.when(pl.program_id(1) == 0)
    def _():
        st_ref[...] = jnp.zeros(st_ref.shape, F32)

    hk, hv = HEADS * GLA_DK, HEADS * GLA_DV
    x = pb_ref[0]
    q = x[:, :hk] * (GLA_DK ** -0.5)
    k = x[:, hk:2 * hk]
    v = x[:, 2 * hk:2 * hk + hv]
    gate = x[:, 2 * hk + hv:2 * hk + 2 * hv]
    logits = _dot(x[:, 2 * hk + 2 * hv:].astype(BF16), wg_ref[...]) + bg_ref[...]
    g = _log_sigmoid(logits) * (1.0 / GLA_TAU)
    _rec_tile(q, k, v, g, st_ref, plo_sc, phi_sc, o_sc, ltri_ref, ind_ref, hmk_ref, hmv_ref, hms_ref)
    y_ref[0] = _head_rms_gate(o_sc[...], gate, gn_ref[...], indv_ref).astype(BF16)


def _hgrn_kernel(pc_ref, lb_ref, gn_ref, ltri_ref, ind_ref, indv_ref, hmk_ref, hmv_ref, hms_ref,
                 y_ref, st_ref, plo_sc, phi_sc, o_sc):
    @pl.when(pl.program_id(1) == 0)
    def _():
        st_ref[...] = jnp.zeros(st_ref.shape, F32)

    hk, hv = HEADS * HGRN_DK, HEADS * HGRN_DV
    x = pc_ref[0]
    cq = x[:, :hk]
    z = x[:, hk:2 * hk]
    v = x[:, 2 * hk:2 * hk + hv]
    gate = x[:, 2 * hk + hv:]
    lb = lb_ref[...]
    f = jnp.maximum(lb, LB_FLOOR) + (1.0 - lb) * _sigmoid(z)
    g = jnp.minimum(jnp.log(f), 0.0)
    k = (1.0 - lb) * _sigmoid(-z)
    q = cq * _sigmoid(cq) * (HGRN_DK ** -0.5)
    _rec_tile(q, k, v, g, st_ref, plo_sc, phi_sc, o_sc, ltri_ref, ind_ref, hmk_ref, hmv_ref, hms_ref)
    y_ref[0] = _head_rms_gate(o_sc[...], gate, gn_ref[...], indv_ref).astype(BF16)


def _recurrence(kernel, name, x, small, consts, dk, dv, ts):
    b, s, width = x.shape
    hk, hv = HEADS * dk, HEADS * dv
    ltri, ind, indv, hmk, hmv, hms = consts
    operands = (x, *small, ltri, ind, indv, hmk, hmv, hms)
    in_specs = [pl.BlockSpec((1, ts, width), lambda i, t: (i, t, 0))]
    in_specs += [_const_spec(a.shape) for a in operands[1:]]
    return pl.pallas_call(
        kernel,
        grid=(b, s // ts),
        in_specs=in_specs,
        out_specs=pl.BlockSpec((1, ts, hv), lambda i, t: (i, t, 0)),
        out_shape=jax.ShapeDtypeStruct((b, s, hv), BF16),
        scratch_shapes=[pltpu.VMEM((hv, hk), F32),
                        pltpu.VMEM((ts // REC_CHUNK, REC_CHUNK // 2, REC_BLOCK // 2 * hk), BF16),
                        pltpu.VMEM((ts // REC_CHUNK, REC_CHUNK // 2, REC_BLOCK * hk), BF16),
                        pltpu.VMEM((ts, hv), F32)],
        compiler_params=_params("arbitrary", "arbitrary"),
        name=name,
    )(*operands)


def _mixca_kernel(h_ref, ya_ref, yb_ref, yc_ref, wa_ref, wb_ref, wc_ref, g_ref, wq_ref, gq_ref,
                  km_ref, vm_ref, wo_ref, out_ref):
    h1 = (h_ref[0] + _dot(ya_ref[0, 0, 0], wa_ref[...]) + _dot(yb_ref[0], wb_ref[...])
          + _dot(yc_ref[0], wc_ref[...]))
    qc = _dot(_rms(h1, g_ref[...]).astype(BF16), wq_ref[...])
    heads = []
    for h in range(HEADS):
        qn = _rms(qc[:, CA_DIM * h:CA_DIM * (h + 1)], gq_ref[...]) * (CA_DIM ** -0.5)
        s = _dot_nt(qn.astype(BF16), km_ref[0, h])
        p = jnp.exp(s - jnp.max(s, axis=-1, keepdims=True))
        o = _dot(p.astype(BF16), vm_ref[0, h]) / jnp.sum(p, axis=-1, keepdims=True)
        heads.append(o.astype(BF16))
    out_ref[0] = h1 + _dot(jnp.concatenate(heads, axis=-1), wo_ref[...])


def _mixca(h, ya, yb, yc, wa, wb, wc, g, wq, gq, km, vm, wo, tm):
    b, s, d = h.shape
    tok = lambda last: pl.BlockSpec((1, tm, last), lambda i, t: (i, t, 0))
    mem = pl.BlockSpec((1,) + km.shape[1:], lambda i, t: (i, 0, 0, 0))
    nq = s // tm
    assert ya.shape[1:4] == (2, nq // 2, tm)
    paired = pl.BlockSpec(
        (1, 1, 1, tm, ya.shape[-1]),
        lambda i, t: (i, t // (nq // 2), jnp.where(t < nq // 2, t, nq - 1 - t), 0, 0))
    return pl.pallas_call(
        _mixca_kernel,
        grid=(b, s // tm),
        in_specs=[tok(d), paired, tok(yb.shape[-1]), tok(yc.shape[-1]),
                  _const_spec(wa.shape), _const_spec(wb.shape), _const_spec(wc.shape),
                  _const_spec(g.shape), _const_spec(wq.shape), _const_spec(gq.shape),
                  mem, mem, _const_spec(wo.shape)],
        out_specs=tok(d),
        out_shape=jax.ShapeDtypeStruct((b, s, d), F32),
        compiler_params=_params("arbitrary", "arbitrary"),
        name="mix_ca",
    )(h, ya, yb, yc, wa, wb, wc, g, wq, gq, km, vm, wo)


def _ffn_kernel(h_ref, g_ref, wg_ref, wu_ref, cw_ref, cb_ref, wo_ref, out_ref, gs_sc, carry_sc, *, bounds):
    tm = h_ref.shape[1]

    @pl.when(pl.program_id(1) == 0)
    def _():
        carry_sc[...] = jnp.zeros(carry_sc.shape, F32)

    x = h_ref[0]
    n = _rms(x, g_ref[...]).astype(BF16)
    acc = x
    for c0, c1 in zip(bounds[:-1], bounds[1:]):
        fc = c1 - c0
        gate = _dot(n, wg_ref[:, c0:c1])
        gs_sc[0:8, :fc] = carry_sc[:, c0:c1]
        gs_sc[8:8 + tm, :fc] = gate
        carry_sc[:, c0:c1] = gate[tm - 8:, :]
        conv = (cb_ref[:, c0:c1] + gs_sc[6:6 + tm, :fc] * cw_ref[0:1, c0:c1]
                + gs_sc[7:7 + tm, :fc] * cw_ref[1:2, c0:c1] + gate * cw_ref[2:3, c0:c1])
        up = _dot(n, wu_ref[:, c0:c1])
        act = (conv * _sigmoid(conv) * up).astype(BF16)
        acc = acc + _dot(act, wo_ref[c0:c1, :])
    out_ref[0] = acc


def _ffn(h, g, wg, wu, cw, cb, wo, tm):
    b, s, d = h.shape
    d_ff = wg.shape[1]
    n_tiles = -(-d_ff // MXU_COLS)
    bounds = (0, min(d_ff, (n_tiles + 1) // 2 * MXU_COLS), d_ff)
    fc = max(bounds[1], d_ff - bounds[1])
    tok = pl.BlockSpec((1, tm, d), lambda i, t: (i, t, 0))
    return pl.pallas_call(
        functools.partial(_ffn_kernel, bounds=bounds),
        grid=(b, s // tm),
        in_specs=[tok, _const_spec(g.shape), _const_spec(wg.shape), _const_spec(wu.shape),
                  _const_spec(cw.shape), _const_spec(cb.shape), _const_spec(wo.shape)],
        out_specs=tok,
        out_shape=jax.ShapeDtypeStruct((b, s, d), F32),
        scratch_shapes=[pltpu.VMEM((tm + 8, fc), F32), pltpu.VMEM((8, d_ff), F32)],
        compiler_params=_params("arbitrary", "arbitrary"),
        name="ffn",
    )(h, g, wg, wu, cw, cb, wo)


def _rope_lanes(x):
    half = MLA_ROPE // 2
    z = jnp.zeros(x.shape[:-1] + (LANES // 2 - half,), x.dtype)
    return jnp.concatenate([x[..., :half], z, x[..., half:], z], axis=-1)


def _rope_lanes_swapped(x):
    half = MLA_ROPE // 2
    return _rope_lanes(jnp.concatenate([x[..., half:], x[..., :half]], axis=-1))


def _qk_lanes(x):
    rope = x[..., MLA_NOPE:]
    return jnp.concatenate([x[..., :MLA_NOPE], _rope_lanes(rope), _rope_lanes_swapped(rope)], axis=-1)


def _relayout_w_in(w):
    offs = np.cumsum([0, MLA_Q_LORA, MLA_KV_LORA, MLA_ROPE, HEADS * GLA_DK, HEADS * GLA_DK, HEADS * GLA_DV,
                      GLA_RANK, HEADS * GLA_DV, HEADS * HGRN_DK, HEADS * HGRN_DK, HEADS * HGRN_DV,
                      HEADS * HGRN_DV])
    seg = lambda i: w[:, offs[i]:offs[i + 1]]
    gate_lr = jnp.pad(seg(6), ((0, 0), (0, LANES - GLA_RANK)))
    return jnp.concatenate([seg(0), seg(1), _rope_lanes(seg(2)), _rope_lanes_swapped(seg(2)),
                            seg(3), seg(4), seg(5), seg(7), gate_lr,
                            seg(8), seg(9), seg(10), seg(11)], axis=-1)


def _pick(n, pref):
    return pref if n % pref == 0 else n


def kernel(x, mem, positions, norm_mix, w_in, mla_q_lat_norm, w_q_up, mla_kv_lat_norm, w_kv_up, mla_q_norm,
           mla_k_norm, gla_w_gate, gla_b_gate, gla_out_norm, hgrn_lb_logits, hgrn_out_norm, w_mix_out,
           norm_ca, norm_mem, w_ca_q, w_ca_kv, ca_q_norm, ca_k_norm, w_ca_out, norm_ffn, w_ffn_in,
           ffn_conv_w, ffn_conv_b, w_ffn_out):
    b, s, d = x.shape
    depth = w_in.shape[0]
    d_ff = w_ffn_out.shape[1]
    tm = _pick(s, 512)
    t_attn = _pick(s, 512)
    ts_rec = _pick(s, 512)

    gamma = jax.nn.softmax(hgrn_lb_logits.astype(F32), axis=0)
    lower_bounds = jnp.cumsum(gamma, axis=0) - gamma[0]
    inv_freq = ROPE_THETA ** (-jnp.arange(0, MLA_ROPE, 2, dtype=F32) / MLA_ROPE)
    ang = positions.astype(F32)[..., None] * inv_freq
    cos = _rope_lanes(jnp.concatenate([jnp.cos(ang), jnp.cos(ang)], axis=-1))
    sin = _rope_lanes(jnp.concatenate([-jnp.sin(ang), jnp.sin(ang)], axis=-1))

    km, vm = _memkv(mem, norm_mem, w_ca_kv.astype(BF16), ca_k_norm)
    gla_consts = _rec_consts(GLA_DK, GLA_DV)
    hgrn_consts = _rec_consts(HGRN_DK, HGRN_DV)
    row = lambda a: a.reshape(1, -1)
    n_a = HEADS * MLA_V
    n_b = HEADS * GLA_DV

    h = x
    for l in range(depth):
        w_l = _relayout_w_in(w_in[l]).astype(BF16)
        wq_l = _qk_lanes(w_q_up[l].reshape(MLA_Q_LORA, HEADS, MLA_QK)).reshape(MLA_Q_LORA, -1).astype(BF16)
        q, k, v, pb, pc = _inproj(
            h, row(norm_mix[l]), w_l, row(mla_q_lat_norm[l]), wq_l, row(mla_kv_lat_norm[l]),
            w_kv_up[l].astype(BF16), row(_qk_lanes(mla_q_norm[l])), row(_qk_lanes(mla_k_norm[l])),
            cos, sin, tm)
        ya = _mla_attn(q, k, v, t_attn)
        wg_l = jnp.pad(gla_w_gate[l], ((0, LANES - GLA_RANK), (0, 0))).astype(BF16)
        yb = _recurrence(_gla_kernel, "gla", pb,
                         (wg_l, row(gla_b_gate[l]), row(jnp.tile(gla_out_norm[l], HEADS))),
                         gla_consts, GLA_DK, GLA_DV, ts_rec)
        yc = _recurrence(_hgrn_kernel, "hgrn", pc,
                         (row(lower_bounds[l]), row(jnp.tile(hgrn_out_norm[l], HEADS))),
                         hgrn_consts, HGRN_DK, HGRN_DV, ts_rec)
        wm = w_mix_out[l].astype(BF16)
        h = _mixca(h, ya, yb, yc, wm[:n_a], wm[n_a:n_a + n_b], wm[n_a + n_b:], row(norm_ca[l]),
                   w_ca_q[l].astype(BF16), row(ca_q_norm[l]), km[l], vm[l], w_ca_out[l].astype(BF16), tm)
        wf = w_ffn_in[l].astype(BF16)
        cw = jnp.pad(ffn_conv_w[l], ((0, 8 - ffn_conv_w.shape[1]), (0, 0)))
        h = _ffn(h, row(norm_ffn[l]), wf[:, :d_ff], wf[:, d_ff:], cw, row(ffn_conv_b[l]),
                 w_ffn_out[l].astype(BF16), tm)
    return h
```

```python
import functools

import numpy as np
import jax
import jax.numpy as jnp
from jax import lax
from jax.experimental import pallas as pl
from jax.experimental.pallas import tpu as pltpu

F32 = jnp.float32
BF16 = jnp.bfloat16

EPS = 1e-6
MASK_VALUE = -1e30
LB_FLOOR = 1e-30
CHUNK = 64
ROPE_THETA = 10000.0
LOG2_E = 1.4426950408889634

HEADS = 4
MLA_Q_LORA = 256
MLA_KV_LORA = 256
MLA_NOPE = 128
MLA_ROPE = 64
MLA_V = 128
MLA_QK = MLA_NOPE + MLA_ROPE
MLA_QK_PAD = 256
GLA_DK = 32
GLA_DV = 64
GLA_RANK = 16
GLA_TAU = 16.0
HGRN_DK = 64
HGRN_DV = 64
CA_DIM = 128
MLA_Q_COLS = 384
N_A = 768
N_B = 896
N_C = 1024

LANES = 128
MXU_COLS = 256
REC_CHUNK = 128
REC_BLOCK = 16
VMEM_LIMIT = 56 * 1024 * 1024

NT_DIMS = (((1,), (1,)), ((), ()))
TN_DIMS = (((0,), (0,)), ((), ()))


def _dot(a, b):
    return jnp.dot(a, b, preferred_element_type=F32)


def _dot_nt(a, b):
    return lax.dot_general(a, b, NT_DIMS, preferred_element_type=F32)


def _rms(x, gain, dim=None):
    dim = x.shape[-1] if dim is None else dim
    ms = jnp.sum(x * x, axis=-1, keepdims=True) * (1.0 / dim)
    return x * lax.rsqrt(ms + EPS) * gain


def _sigmoid(x):
    return 1.0 / (1.0 + jnp.exp(-x))


def _log_sigmoid(x):
    return jnp.minimum(x, 0.0) - jnp.log1p(jnp.exp(-jnp.abs(x)))


def _params(*sem):
    return pltpu.CompilerParams(dimension_semantics=sem, vmem_limit_bytes=VMEM_LIMIT)


def _const_spec(shape):
    nd = len(shape)
    return pl.BlockSpec(shape, lambda *_: (0,) * nd)


def _memkv_kernel(mem_ref, g_ref, w_ref, gk_ref, k_ref, v_ref):
    n = _rms(mem_ref[0], g_ref[0]).astype(BF16)
    kv = _dot(n, w_ref[0])
    for h in range(HEADS):
        base = 2 * CA_DIM * h
        k_ref[0, 0, h] = _rms(kv[:, base:base + CA_DIM], gk_ref[0]).astype(BF16)
        v_ref[0, 0, h] = kv[:, base + CA_DIM:base + 2 * CA_DIM].astype(BF16)


def _memkv(mem, norm_mem, w_ca_kv, ca_k_norm):
    depth, d = norm_mem.shape
    b, m, _ = mem.shape
    out = jax.ShapeDtypeStruct((depth, b, HEADS, m, CA_DIM), BF16)
    spec_o = pl.BlockSpec((1, 1, HEADS, m, CA_DIM), lambda l, i: (l, i, 0, 0, 0))
    return pl.pallas_call(
        _memkv_kernel,
        grid=(depth, b),
        in_specs=[
            pl.BlockSpec((1, m, d), lambda l, i: (i, 0, 0)),
            pl.BlockSpec((1, 1, d), lambda l, i: (l, 0, 0)),
            pl.BlockSpec((1, d, w_ca_kv.shape[-1]), lambda l, i: (l, 0, 0)),
            pl.BlockSpec((1, 1, CA_DIM), lambda l, i: (l, 0, 0)),
        ],
        out_specs=[spec_o, spec_o],
        out_shape=[out, out],
        compiler_params=_params("arbitrary", "arbitrary"),
        name="memkv",
    )(mem, norm_mem.reshape(depth, 1, d), w_ca_kv, ca_k_norm.reshape(depth, 1, CA_DIM))


def _inproj_kernel(h_ref, g_ref, w_ref, gql_ref, wq_ref, gkvl_ref, wkv_ref, gq_ref, gk_ref,
                   cos_ref, sin_ref, ones_ref, bd_ref, q_ref, k_ref, v_ref, pb_ref, pc_ref):
    n = _rms(h_ref[0], g_ref[...]).astype(BF16)
    pa = _dot(n, w_ref[:, :N_A])
    pb_ref[0] = _dot(n, w_ref[:, N_A:N_A + N_B])
    pc_ref[0] = _dot(n, w_ref[:, N_A + N_B:])

    def latent(x, gain):
        ms = _dot((x * x).astype(BF16), ones_ref[...]) * (1.0 / x.shape[1])
        return (x * lax.rsqrt(ms + EPS) * gain).astype(BF16)

    qf = _dot(latent(pa[:, :MLA_Q_LORA], gql_ref[...]), wq_ref[...])
    kvf = _dot(latent(pa[:, MLA_Q_LORA:MLA_Q_LORA + MLA_KV_LORA], gkvl_ref[...]), wkv_ref[...])
    kr = pa[:, N_A - 2 * LANES:N_A - LANES]
    kr_sw = pa[:, N_A - LANES:]
    cos = cos_ref[0]
    sin = sin_ref[0]
    gq = gq_ref[...]
    gk = gk_ref[...]
    scale = MLA_QK ** -0.5 * LOG2_E
    gq_nope = gq[:, :LANES] * scale
    q_cos = gq[:, LANES:2 * LANES] * cos * scale
    q_sin = gq[:, 2 * LANES:] * sin * scale
    k_rot = kr * (gk[:, LANES:2 * LANES] * cos) + kr_sw * (gk[:, 2 * LANES:] * sin)
    kr_sq = kr * kr
    q_sq = jnp.concatenate(
        [qf[:, MLA_Q_COLS * h:MLA_Q_COLS * h + LANES] ** 2
         + qf[:, MLA_Q_COLS * h + LANES:MLA_Q_COLS * h + 2 * LANES] ** 2 for h in range(HEADS)], axis=1)
    k_sq = jnp.concatenate(
        [kvf[:, 2 * LANES * h:2 * LANES * h + LANES] ** 2 + kr_sq for h in range(HEADS)], axis=1)
    q_rs = lax.rsqrt(_dot(q_sq.astype(BF16), bd_ref[...]) * (1.0 / MLA_QK) + EPS)
    k_rs = lax.rsqrt(_dot(k_sq.astype(BF16), bd_ref[...]) * (1.0 / MLA_QK) + EPS)
    for h in range(HEADS):
        qb = MLA_Q_COLS * h
        rs = q_rs[:, LANES * h:LANES * (h + 1)]
        q_ref[0, h, :, :MLA_NOPE] = (qf[:, qb:qb + LANES] * rs * gq_nope).astype(BF16)
        q_ref[0, h, :, MLA_NOPE:] = ((qf[:, qb + LANES:qb + 2 * LANES] * q_cos
                                      + qf[:, qb + 2 * LANES:qb + 3 * LANES] * q_sin) * rs).astype(BF16)
        kb = 2 * LANES * h
        rs = k_rs[:, LANES * h:LANES * (h + 1)]
        k_ref[0, h, :, :MLA_NOPE] = (kvf[:, kb:kb + LANES] * rs * gk[:, :LANES]).astype(BF16)
        k_ref[0, h, :, MLA_NOPE:] = (k_rot * rs).astype(BF16)
        v_ref[0, h] = kvf[:, kb + LANES:kb + 2 * LANES].astype(BF16)


def _inproj(h, g, w, gql, wq, gkvl, wkv, gq, gk, cos, sin, tm):
    ones = jnp.ones((MLA_Q_LORA, MLA_Q_LORA), BF16)
    head_of = np.arange(HEADS * LANES) // LANES
    bd = jnp.asarray(head_of[:, None] == head_of[None, :], BF16)
    b, s, d = h.shape
    grid = (b, s // tm)
    tok = lambda last: pl.BlockSpec((1, tm, last), lambda i, t: (i, t, 0))
    head = lambda last: pl.BlockSpec((1, HEADS, tm, last), lambda i, t: (i, 0, t, 0))
    return pl.pallas_call(
        _inproj_kernel,
        grid=grid,
        in_specs=[tok(d), _const_spec(g.shape), _const_spec(w.shape), _const_spec(gql.shape),
                  _const_spec(wq.shape), _const_spec(gkvl.shape), _const_spec(wkv.shape),
                  _const_spec(gq.shape), _const_spec(gk.shape), tok(LANES), tok(LANES),
                  _const_spec(ones.shape), _const_spec(bd.shape)],
        out_specs=[head(MLA_QK_PAD), head(MLA_QK_PAD), head(MLA_V), tok(N_B), tok(N_C)],
        out_shape=[jax.ShapeDtypeStruct((b, HEADS, s, MLA_QK_PAD), BF16),
                   jax.ShapeDtypeStruct((b, HEADS, s, MLA_QK_PAD), BF16),
                   jax.ShapeDtypeStruct((b, HEADS, s, MLA_V), BF16),
                   jax.ShapeDtypeStruct((b, s, N_B), F32),
                   jax.ShapeDtypeStruct((b, s, N_C), F32)],
        compiler_params=_params("arbitrary", "arbitrary"),
        name="inproj",
    )(h, g, w, gql, wq, gkvl, wkv, gq, gk, cos, sin, ones, bd)


def _mla_attn_kernel(qa_ref, qb_ref, k_ref, v_ref, o_ref, q_sc, m_sc, l_sc, acc_sc, sa_sc, sb_sc, *, t, nq):
    g = pl.program_id(2)
    q_sc[0:t, :] = qa_ref[0, 0]
    q_sc[t:2 * t, :] = qb_ref[0, 0]
    m_sc[...] = jnp.full(m_sc.shape, MASK_VALUE, F32)
    l_sc[...] = jnp.zeros(l_sc.shape, F32)
    acc_sc[...] = jnp.zeros(acc_sc.shape, F32)

    def scores(rows, kt):
        return _dot_nt(q_sc[rows, :], k_ref[0, 0, pl.ds(pl.multiple_of(kt * t, t), t), :])

    def update(s, rows, kt):
        m_prev = m_sc[rows, :]
        m_new = jnp.maximum(m_prev, jnp.max(s, axis=-1, keepdims=True))
        alpha = jnp.exp2(m_prev - m_new)
        p = jnp.exp2(s - jnp.concatenate([m_new] * (t // LANES), axis=1))
        l_sc[rows, :] = alpha * l_sc[rows, :] + jnp.sum(p, axis=-1, keepdims=True)
        v_tile = v_ref[0, 0, pl.ds(pl.multiple_of(kt * t, t), t), :]
        acc_sc[rows, :] = alpha * acc_sc[rows, :] + _dot(p.astype(BF16), v_tile)
        m_sc[rows, :] = m_new

    qc = lax.broadcasted_iota(jnp.int32, (t, t), 0) // CHUNK
    kc = lax.broadcasted_iota(jnp.int32, (t, t), 1) // CHUNK
    bufs = (sa_sc, sb_sc)
    rows_a, rows_b = pl.ds(0, t), pl.ds(t, t)

    work = []
    for j in range(nq - 1):
        if j < nq // 2 - 1:
            sel = (j >= g).astype(jnp.int32)
            work.append((pl.ds(pl.multiple_of(sel * t, t), t), j - sel * g, False))
        else:
            work.append((rows_b, j - g, False))
    work += [(rows_a, g, True), (rows_b, nq - 1 - g, True)]
    sa_sc[...] = scores(work[0][0], work[0][1])
    for i, (rows, kt, diagonal) in enumerate(work):
        if i + 1 < len(work):
            bufs[(i + 1) % 2][...] = scores(work[i + 1][0], work[i + 1][1])
        s = bufs[i % 2][...]
        if diagonal:
            s = jnp.where(kc <= qc, s, MASK_VALUE)
        update(s, rows, kt)

    o = (acc_sc[...] / l_sc[...]).astype(BF16)
    o_ref[0, 0, 0] = o[:t]
    o_ref[0, 1, 0] = o[t:]


def _mla_attn(q, k, v, t):
    b, nh, s, dq = q.shape
    dv = v.shape[-1]
    nq = s // t
    half = nq // 2
    return pl.pallas_call(
        functools.partial(_mla_attn_kernel, t=t, nq=nq),
        grid=(b, nh, half),
        in_specs=[pl.BlockSpec((1, 1, t, dq), lambda i, h, j: (i, h, j, 0)),
                  pl.BlockSpec((1, 1, t, dq), lambda i, h, j: (i, h, nq - 1 - j, 0)),
                  pl.BlockSpec((1, 1, s, dq), lambda i, h, j: (i, h, 0, 0)),
                  pl.BlockSpec((1, 1, s, dv), lambda i, h, j: (i, h, 0, 0))],
        out_specs=pl.BlockSpec((1, 2, 1, t, dv), lambda i, h, j: (i, 0, j, 0, h)),
        out_shape=jax.ShapeDtypeStruct((b, 2, half, t, nh * dv), BF16),
        scratch_shapes=[pltpu.VMEM((2 * t, dq), BF16), pltpu.VMEM((2 * t, LANES), F32),
                        pltpu.VMEM((2 * t, LANES), F32), pltpu.VMEM((2 * t, dv), F32),
                        pltpu.VMEM((t, t), F32), pltpu.VMEM((t, t), F32)],
        compiler_params=_params("arbitrary", "arbitrary", "arbitrary"),
        name="mla_attn",
    )(q, q, k, v)


def _rec_consts(dk, dv):
    hk, hv = HEADS * dk, HEADS * dv
    c, blk = REC_CHUNK, REC_BLOCK
    head_k = np.arange(hk) // dk
    head_v = np.arange(hv) // dv
    head_r = np.repeat(np.arange(HEADS), blk)
    ltri = (np.arange(c)[:, None] >= np.arange(c)[None, :])
    ind = head_k[:, None] == head_v[None, :]
    col = (head_k[None, :] * blk + np.arange(blk)[:, None]).reshape(-1)
    ind_diag = col[:, None] == np.arange(HEADS * blk)[None, :]
    indv = (head_v[:, None] == head_v[None, :]) / dv
    hm_k = head_r[:, None] == head_k[None, :]
    hm_v = head_r[:, None] == head_v[None, :]
    return (jnp.asarray(ltri, BF16), jnp.asarray(ind_diag, BF16), jnp.asarray(indv, BF16),
            jnp.asarray(hm_k, F32), jnp.asarray(hm_v, F32), jnp.asarray(ind.T, F32))


def _rec_tile(q, k, v, g, st_ref, plo_sc, phi_sc, o_sc, ltri_ref, ind_ref, hmk_ref, hmv_ref, hms_ref):
    ts, hk = q.shape
    hv = v.shape[1]
    c, blk = REC_CHUNK, REC_BLOCK
    nb = c // blk
    half = blk // 2
    chunks = [slice(i * c, (i + 1) * c) for i in range(ts // c)]
    g2 = g * LOG2_E
    g_hi = g2.astype(BF16)
    g_lo = (g2 - g_hi.astype(F32)).astype(BF16)
    ltri = ltri_ref[...]
    b = [_dot(ltri, g_hi[r]) + _dot(ltri, g_lo[r]) for r in chunks]

    sub = lax.broadcasted_iota(jnp.int32, (nb, half, hk), 1)
    diag = []
    for ci, r in enumerate(chunks):
        b4 = b[ci].reshape(nb, 2, half, hk)
        q4 = q[r].reshape(nb, 2, half, hk)
        k4 = k[r].reshape(nb, 2, half, hk)
        for jj in range(blk):
            hf, jr = divmod(jj, half)
            bj = b4[:, hf, jr:jr + 1, :]
            kj = k4[:, hf, jr:jr + 1, :]
            p = jnp.where(sub >= jr, q4[:, hf] * kj * jnp.exp2(b4[:, hf] - bj), 0.0)
            if hf == 0:
                plo_sc[ci, :, jj * hk:(jj + 1) * hk] = p.reshape(c // 2, hk).astype(BF16)
                p = q4[:, 1] * kj * jnp.exp2(b4[:, 1] - bj)
            phi_sc[ci, :, jj * hk:(jj + 1) * hk] = p.reshape(c // 2, hk).astype(BF16)
        d_lo = _dot(plo_sc[ci], ind_ref[:half * hk, :]).reshape(nb, half, HEADS * blk)
        d_hi = _dot(phi_sc[ci], ind_ref[...]).reshape(nb, half, HEADS * blk)
        diag.append(jnp.stack([d_lo, d_hi], axis=1).reshape(c, HEADS * blk))

    hmk = hmk_ref[...]
    hmv = hmv_ref[...]
    for jb in range(nb):
        lo, hi = blk * jb, blk * (jb + 1)
        for ci, r in enumerate(chunks):
            qc, kc, vc, bc = q[r], k[r], v[r], b[ci]
            a = diag[ci][lo:hi, :]
            if hi < c:
                b_end = bc[hi - 1:hi, :]
                qj = (qc[hi:, :] * jnp.exp2(bc[hi:, :] - b_end)).astype(BF16)
                kj = kc[lo:hi, :] * jnp.exp2(b_end - bc[lo:hi, :])
                kbd = (jnp.concatenate([kj] * HEADS, axis=0) * hmk).astype(BF16)
                a = jnp.concatenate([a, _dot_nt(qj, kbd)], axis=0)
            vbd = (jnp.concatenate([vc[lo:hi, :]] * HEADS, axis=0) * hmv).astype(BF16)
            contrib = _dot(a.astype(BF16), vbd)
            if jb == 0:
                o_sc[r, :] = contrib
            else:
                o_sc[r.start + lo:r.stop, :] += contrib

    upd = []
    for ci, r in enumerate(chunks):
        kh = (k[r] * jnp.exp2(b[ci][c - 1:c, :] - b[ci])).astype(BF16)
        upd.append(lax.dot_general(v[r].astype(BF16), kh, TN_DIMS, preferred_element_type=F32))
    st = st_ref[...]
    for ci, r in enumerate(chunks):
        o_sc[r, :] += _dot_nt((q[r] * jnp.exp2(b[ci])).astype(BF16), st.astype(BF16))
        st = st * jnp.exp2(b[ci][c - 1:c, :]) + upd[ci] * hms_ref[...]
    st_ref[...] = st


def _head_rms_gate(o, gate, gain, indv_ref):
    ms = _dot((o * o).astype(BF16), indv_ref[...])
    return o * lax.rsqrt(ms + EPS) * gain * (gate * _sigmoid(gate))


def _gla_kernel(pb_ref, wg_ref, bg_ref, gn_ref, ltri_ref, ind_ref, indv_ref, hmk_ref, hmv_ref, hms_ref,
                y_ref, st_ref, plo_sc, phi_sc, o_sc):
    @pl.when(pl.program_id(1) == 0)
    def _():
        st_ref[...] = jnp.zeros(st_ref.shape, F32)

    hk, hv = HEADS * GLA_DK, HEADS * GLA_DV
    x = pb_ref[0]
    q = x[:, :hk] * (GLA_DK ** -0.5)
    k = x[:, hk:2 * hk]
    v = x[:, 2 * hk:2 * hk + hv]
    gate = x[:, 2 * hk + hv:2 * hk + 2 * hv]
    logits = _dot(x[:, 2 * hk + 2 * hv:].astype(BF16), wg_ref[...]) + bg_ref[...]
    g = _log_sigmoid(logits) * (1.0 / GLA_TAU)
    _rec_tile(q, k, v, g, st_ref, plo_sc, phi_sc, o_sc, ltri_ref, ind_ref, hmk_ref, hmv_ref, hms_ref)
    y_ref[0] = _head_rms_gate(o_sc[...], gate, gn_ref[...], indv_ref).astype(BF16)


def _hgrn_kernel(pc_ref, lb_ref, gn_ref, ltri_ref, ind_ref, indv_ref, hmk_ref, hmv_ref, hms_ref,
                 y_ref, st_ref, plo_sc, phi_sc, o_sc):
    @pl.when(pl.program_id(1) == 0)
    def _():
        st_ref[...] = jnp.zeros(st_ref.shape, F32)

    hk, hv = HEADS * HGRN_DK, HEADS * HGRN_DV
    x = pc_ref[0]
    cq = x[:, :hk]
    z = x[:, hk:2 * hk]
    v = x[:, 2 * hk:2 * hk + hv]
    gate = x[:, 2 * hk + hv:]
    lb = lb_ref[...]
    f = jnp.maximum(lb, LB_FLOOR) + (1.0 - lb) * _sigmoid(z)
    g = jnp.minimum(jnp.log(f), 0.0)
    k = (1.0 - lb) * _sigmoid(-z)
    q = cq * _sigmoid(cq) * (HGRN_DK ** -0.5)
    _rec_tile(q, k, v, g, st_ref, plo_sc, phi_sc, o_sc, ltri_ref, ind_ref, hmk_ref, hmv_ref, hms_ref)
    y_ref[0] = _head_rms_gate(o_sc[...], gate, gn_ref[...], indv_ref).astype(BF16)


def _recurrence(kernel, name, x, small, consts, dk, dv, ts):
    b, s, width = x.shape
    hk, hv = HEADS * dk, HEADS * dv
    ltri, ind, indv, hmk, hmv, hms = consts
    operands = (x, *small, ltri, ind, indv, hmk, hmv, hms)
    in_specs = [pl.BlockSpec((1, ts, width), lambda i, t: (i, t, 0))]
    in_specs += [_const_spec(a.shape) for a in operands[1:]]
    return pl.pallas_call(
        kernel,
        grid=(b, s // ts),
        in_specs=in_specs,
        out_specs=pl.BlockSpec((1, ts, hv), lambda i, t: (i, t, 0)),
        out_shape=jax.ShapeDtypeStruct((b, s, hv), BF16),
        scratch_shapes=[pltpu.VMEM((hv, hk), F32),
                        pltpu.VMEM((ts // REC_CHUNK, REC_CHUNK // 2, REC_BLOCK // 2 * hk), BF16),
                        pltpu.VMEM((ts // REC_CHUNK, REC_CHUNK // 2, REC_BLOCK * hk), BF16),
                        pltpu.VMEM((ts, hv), F32)],
        compiler_params=_params("arbitrary", "arbitrary"),
        name=name,
    )(*operands)


def _mixca_kernel(h_ref, ya_ref, yb_ref, yc_ref, wa_ref, wb_ref, wc_ref, g_ref, wq_ref, gq_ref,
                  km_ref, vm_ref, wo_ref, bd_ref, out_ref):
    h1 = (h_ref[0] + _dot(ya_ref[0, 0, 0], wa_ref[...]) + _dot(yb_ref[0], wb_ref[...])
          + _dot(yc_ref[0], wc_ref[...]))
    qc = _dot(_rms(h1, g_ref[...]).astype(BF16), wq_ref[...])
    q_rs = lax.rsqrt(_dot((qc * qc).astype(BF16), bd_ref[...]) * (1.0 / CA_DIM) + EPS)
    heads = []
    for h in range(HEADS):
        cols = slice(CA_DIM * h, CA_DIM * (h + 1))
        qn = qc[:, cols] * q_rs[:, cols] * (gq_ref[...] * (CA_DIM ** -0.5))
        s = _dot_nt(qn.astype(BF16), km_ref[0, h])
        p = jnp.exp(s - jnp.max(s, axis=-1, keepdims=True))
        o = _dot(p.astype(BF16), vm_ref[0, h]) / jnp.sum(p, axis=-1, keepdims=True)
        heads.append(o.astype(BF16))
    out_ref[0] = h1 + _dot(jnp.concatenate(heads, axis=-1), wo_ref[...])


def _mixca(h, ya, yb, yc, wa, wb, wc, g, wq, gq, km, vm, wo, tm):
    b, s, d = h.shape
    tok = lambda last: pl.BlockSpec((1, tm, last), lambda i, t: (i, t, 0))
    mem = pl.BlockSpec((1,) + km.shape[1:], lambda i, t: (i, 0, 0, 0))
    nq = s // tm
    assert ya.shape[1:4] == (2, nq // 2, tm)
    head_of = np.arange(HEADS * CA_DIM) // CA_DIM
    bd = jnp.asarray(head_of[:, None] == head_of[None, :], BF16)
    paired = pl.BlockSpec(
        (1, 1, 1, tm, ya.shape[-1]),
        lambda i, t: (i, t // (nq // 2), jnp.where(t < nq // 2, t, nq - 1 - t), 0, 0))
    return pl.pallas_call(
        _mixca_kernel,
        grid=(b, s // tm),
        in_specs=[tok(d), paired, tok(yb.shape[-1]), tok(yc.shape[-1]),
                  _const_spec(wa.shape), _const_spec(wb.shape), _const_spec(wc.shape),
                  _const_spec(g.shape), _const_spec(wq.shape), _const_spec(gq.shape),
                  mem, mem, _const_spec(wo.shape), _const_spec(bd.shape)],
        out_specs=tok(d),
        out_shape=jax.ShapeDtypeStruct((b, s, d), F32),
        compiler_params=_params("arbitrary", "arbitrary"),
        name="mix_ca",
    )(h, ya, yb, yc, wa, wb, wc, g, wq, gq, km, vm, wo, bd)


def _ffn_kernel(h_ref, g_ref, wg_ref, wu_ref, cw_ref, cb_ref, wo_ref, out_ref, gs_sc, carry_sc, *, bounds):
    tm = h_ref.shape[1]

    @pl.when(pl.program_id(1) == 0)
    def _():
        carry_sc[...] = jnp.zeros(carry_sc.shape, F32)

    x = h_ref[0]
    n = _rms(x, g_ref[...]).astype(BF16)
    acc = x
    for c0, c1 in zip(bounds[:-1], bounds[1:]):
        fc = c1 - c0
        gate = _dot(n, wg_ref[:, c0:c1])
        gs_sc[0:8, :fc] = carry_sc[:, c0:c1]
        gs_sc[8:8 + tm, :fc] = gate
        carry_sc[:, c0:c1] = gate[tm - 8:, :]
        conv = (cb_ref[:, c0:c1] + gs_sc[6:6 + tm, :fc] * cw_ref[0:1, c0:c1]
                + gs_sc[7:7 + tm, :fc] * cw_ref[1:2, c0:c1] + gate * cw_ref[2:3, c0:c1])
        up = _dot(n, wu_ref[:, c0:c1])
        act = (conv * _sigmoid(conv) * up).astype(BF16)
        acc = acc + _dot(act, wo_ref[c0:c1, :])
    out_ref[0] = acc


def _ffn(h, g, wg, wu, cw, cb, wo, tm):
    b, s, d = h.shape
    d_ff = wg.shape[1]
    n_tiles = -(-d_ff // MXU_COLS)
    bounds = (0, min(d_ff, (n_tiles + 1) // 2 * MXU_COLS), d_ff)
    fc = max(bounds[1], d_ff - bounds[1])
    tok = pl.BlockSpec((1, tm, d), lambda i, t: (i, t, 0))
    return pl.pallas_call(
        functools.partial(_ffn_kernel, bounds=bounds),
        grid=(b, s // tm),
        in_specs=[tok, _const_spec(g.shape), _const_spec(wg.shape), _const_spec(wu.shape),
                  _const_spec(cw.shape), _const_spec(cb.shape), _const_spec(wo.shape)],
        out_specs=tok,
        out_shape=jax.ShapeDtypeStruct((b, s, d), F32),
        scratch_shapes=[pltpu.VMEM((tm + 8, fc), F32), pltpu.VMEM((8, d_ff), F32)],
        compiler_params=_params("arbitrary", "arbitrary"),
        name="ffn",
    )(h, g, wg, wu, cw, cb, wo)


def _rope_lanes(x):
    half = MLA_ROPE // 2
    z = jnp.zeros(x.shape[:-1] + (LANES // 2 - half,), x.dtype)
    return jnp.concatenate([x[..., :half], z, x[..., half:], z], axis=-1)


def _rope_lanes_swapped(x):
    half = MLA_ROPE // 2
    return _rope_lanes(jnp.concatenate([x[..., half:], x[..., :half]], axis=-1))


def _qk_lanes(x):
    rope = x[..., MLA_NOPE:]
    return jnp.concatenate([x[..., :MLA_NOPE], _rope_lanes(rope), _rope_lanes_swapped(rope)], axis=-1)


def _relayout_w_in(w):
    offs = np.cumsum([0, MLA_Q_LORA, MLA_KV_LORA, MLA_ROPE, HEADS * GLA_DK, HEADS * GLA_DK, HEADS * GLA_DV,
                      GLA_RANK, HEADS * GLA_DV, HEADS * HGRN_DK, HEADS * HGRN_DK, HEADS * HGRN_DV,
                      HEADS * HGRN_DV])
    seg = lambda i: w[:, offs[i]:offs[i + 1]]
    gate_lr = jnp.pad(seg(6), ((0, 0), (0, LANES - GLA_RANK)))
    return jnp.concatenate([seg(0), seg(1), _rope_lanes(seg(2)), _rope_lanes_swapped(seg(2)),
                            seg(3), seg(4), seg(5), seg(7), gate_lr,
                            seg(8), seg(9), seg(10), seg(11)], axis=-1)


def _pick(n, pref):
    return pref if n % pref == 0 else n


def kernel(x, mem, positions, norm_mix, w_in, mla_q_lat_norm, w_q_up, mla_kv_lat_norm, w_kv_up, mla_q_norm,
           mla_k_norm, gla_w_gate, gla_b_gate, gla_out_norm, hgrn_lb_logits, hgrn_out_norm, w_mix_out,
           norm_ca, norm_mem, w_ca_q, w_ca_kv, ca_q_norm, ca_k_norm, w_ca_out, norm_ffn, w_ffn_in,
           ffn_conv_w, ffn_conv_b, w_ffn_out):
    b, s, d = x.shape
    depth = w_in.shape[0]
    d_ff = w_ffn_out.shape[1]
    tm = _pick(s, 512)
    t_attn = _pick(s, 512)
    ts_rec = _pick(s, 512)

    gamma = jax.nn.softmax(hgrn_lb_logits.astype(F32), axis=0)
    lower_bounds = jnp.cumsum(gamma, axis=0) - gamma[0]
    inv_freq = ROPE_THETA ** (-jnp.arange(0, MLA_ROPE, 2, dtype=F32) / MLA_ROPE)
    ang = positions.astype(F32)[..., None] * inv_freq
    cos = _rope_lanes(jnp.concatenate([jnp.cos(ang), jnp.cos(ang)], axis=-1))
    sin = _rope_lanes(jnp.concatenate([-jnp.sin(ang), jnp.sin(ang)], axis=-1))

    km, vm = _memkv(mem, norm_mem, w_ca_kv.astype(BF16), ca_k_norm)
    gla_consts = _rec_consts(GLA_DK, GLA_DV)
    hgrn_consts = _rec_consts(HGRN_DK, HGRN_DV)
    row = lambda a: a.reshape(1, -1)
    n_a = HEADS * MLA_V
    n_b = HEADS * GLA_DV

    h = x
    for l in range(depth):
        w_l = _relayout_w_in(w_in[l]).astype(BF16)
        wq_l = _qk_lanes(w_q_up[l].reshape(MLA_Q_LORA, HEADS, MLA_QK)).reshape(MLA_Q_LORA, -1).astype(BF16)
        q, k, v, pb, pc = _inproj(
            h, row(norm_mix[l]), w_l, row(mla_q_lat_norm[l]), wq_l, row(mla_kv_lat_norm[l]),
            w_kv_up[l].astype(BF16), row(_qk_lanes(mla_q_norm[l])), row(_qk_lanes(mla_k_norm[l])),
            cos, sin, tm)
        ya = _mla_attn(q, k, v, t_attn)
        wg_l = jnp.pad(gla_w_gate[l], ((0, LANES - GLA_RANK), (0, 0))).astype(BF16)
        yb = _recurrence(_gla_kernel, "gla", pb,
                         (wg_l, row(gla_b_gate[l]), row(jnp.tile(gla_out_norm[l], HEADS))),
                         gla_consts, GLA_DK, GLA_DV, ts_rec)
        yc = _recurrence(_hgrn_kernel, "hgrn", pc,
                         (row(lower_bounds[l]), row(jnp.tile(hgrn_out_norm[l], HEADS))),
                         hgrn_consts, HGRN_DK, HGRN_DV, ts_rec)
        wm = w_mix_out[l].astype(BF16)
        h = _mixca(h, ya, yb, yc, wm[:n_a], wm[n_a:n_a + n_b], wm[n_a + n_b:], row(norm_ca[l]),
                   w_ca_q[l].astype(BF16), row(ca_q_norm[l]), km[l], vm[l], w_ca_out[l].astype(BF16), tm)
        wf = w_ffn_in[l].astype(BF16)
        cw = jnp.pad(ffn_conv_w[l], ((0, 8 - ffn_conv_w.shape[1]), (0, 0)))
        h = _ffn(h, row(norm_ffn[l]), wf[:, :d_ff], wf[:, d_ff:], cw, row(ffn_conv_b[l]),
                 w_ffn_out[l].astype(BF16), tm)
    return h
```
